```python
import jax, jax.numpy as jnp
from jax import lax
import numpy as np

D_MODEL = 2048
BATCH = 2
SEQ = 8192
DEPTH = 1

CHUNK = 64
Q_BLOCK = 128
FOX_HEADS = 8
FOX_HEAD_DIM = 128
FOX_W = FOX_HEADS * FOX_HEAD_DIM
MLSTM_HEADS = 4
MLSTM_QK_DIM = 128
MLSTM_V_DIM = 256
ML_QK_W = MLSTM_HEADS * MLSTM_QK_DIM
ML_V_W = MLSTM_HEADS * MLSTM_V_DIM
IGATE_CAP = 15.0
N_EXPERTS = 32
TOP_K = 4
D_FF = 2048
SWIGLU_LIMIT = 7.0
SWIGLU_ALPHA = 1.702
EPS = 1e-5

IN_WIDTHS = (FOX_W, FOX_W, FOX_W, FOX_HEADS,
             ML_QK_W, ML_QK_W, ML_V_W, MLSTM_HEADS, MLSTM_HEADS, ML_V_W,
             D_MODEL, D_MODEL)
IN_COLS = sum(IN_WIDTHS)

kernel_name = "fox_mlstm_gated_moe_block"


def rms_norm(x, g):
    xf = x.astype(jnp.float32)
    return xf * lax.rsqrt(jnp.mean(xf * xf, axis=-1, keepdims=True) + EPS) * g.astype(jnp.float32)


def forgetting_attention(q, k, v, log_f):
    bsz, seq, _, hd = q.shape
    q = jnp.transpose(q, (0, 2, 1, 3))
    k = jnp.transpose(k, (0, 2, 1, 3))
    v = jnp.transpose(v, (0, 2, 1, 3))
    cum_f = jnp.cumsum(jnp.transpose(log_f, (0, 2, 1)), axis=-1)
    scale = hd ** -0.5
    outs = []
    for blk in range(seq // Q_BLOCK):
        q0 = blk * Q_BLOCK
        q1 = q0 + Q_BLOCK
        logits = jnp.einsum('bhqd,bhkd->bhqk', q[:, :, q0:q1], k[:, :, :q1]) * scale
        logits = logits + cum_f[:, :, q0:q1, None] - cum_f[:, :, None, :q1]
        causal = jnp.arange(q0, q1)[:, None] >= jnp.arange(q1)[None, :]
        logits = jnp.where(causal, logits, -jnp.inf)
        probs = jax.nn.softmax(logits, axis=-1)
        outs.append(jnp.einsum('bhqk,bhkd->bhqd', probs, v[:, :, :q1]))
    out = jnp.concatenate(outs, axis=2)
    return jnp.transpose(out, (0, 2, 1, 3)).reshape(bsz, seq, -1)


def mlstm_chunkwise(q, k, v, i_pre, log_f):
    bsz, seq, nh, dk = q.shape
    dv = v.shape[-1]
    n_chunks = seq // CHUNK

    def to_chunks(a):
        a = a.reshape((bsz, n_chunks, CHUNK) + a.shape[2:])
        return jnp.moveaxis(jnp.moveaxis(a, 1, 0), 3, 2)

    qc = to_chunks(q)
    kc = to_chunks(k * (dk ** -0.5))
    vc = to_chunks(v)
    ic = to_chunks(i_pre)
    fc = to_chunks(log_f)
    tril = jnp.tril(jnp.ones((CHUNK, CHUNK), dtype=bool))

    def step(carry, inp):
        c_st, n_st, m_st = carry
        q_c, k_c, v_c, i_c, f_c = inp
        b = jnp.cumsum(f_c, axis=-1)
        d_log = b[..., :, None] - b[..., None, :] + i_c[..., None, :]
        d_log = jnp.where(tril, d_log, -jnp.inf)
        g_inter = b + m_st[..., None]
        m_row = jnp.maximum(g_inter, jnp.max(d_log, axis=-1))
        w_intra = jnp.exp(d_log - m_row[..., None])
        w_inter = jnp.exp(g_inter - m_row)
        scores = jnp.einsum('bhtd,bhsd->bhts', q_c, k_c) * w_intra
        num = (w_inter[..., None] * jnp.einsum('bhtd,bhdv->bhtv', q_c, c_st)
               + jnp.einsum('bhts,bhsv->bhtv', scores, v_c))
        den = w_inter * jnp.einsum('bhtd,bhd->bht', q_c, n_st) + jnp.sum(scores, axis=-1)
        h = num / jnp.maximum(jnp.abs(den), jnp.exp(-m_row))[..., None]
        b_last = b[..., -1]
        a_log = b_last[..., None] - b + i_c
        m_new = jnp.maximum(b_last + m_st, jnp.max(a_log, axis=-1))
        decay = jnp.exp(b_last + m_st - m_new)
        w_upd = jnp.exp(a_log - m_new[..., None])
        c_new = decay[..., None, None] * c_st + jnp.einsum('bhs,bhsd,bhsv->bhdv', w_upd, k_c, v_c)
        n_new = decay[..., None] * n_st + jnp.einsum('bhs,bhsd->bhd', w_upd, k_c)
        return (c_new, n_new, m_new), h

    init = (jnp.zeros((bsz, nh, dk, dv), jnp.float32),
            jnp.zeros((bsz, nh, dk), jnp.float32),
            jnp.zeros((bsz, nh), jnp.float32))
    _, hs = lax.scan(step, init, (qc, kc, vc, ic, fc))
    hs = jnp.moveaxis(jnp.moveaxis(hs, 2, 3), 0, 1)
    return hs.reshape(bsz, seq, nh, dv)


def setup_inputs(seed: int = 0) -> dict:
    key = jax.random.key(seed)
    ks = jax.random.split(key, 22)
    f32 = jnp.float32
    nrm = lambda k, shape, s: jax.random.normal(k, shape, f32) * s
    return {
        "x": nrm(ks[0], (BATCH, SEQ, D_MODEL), 1.0),
        "norm1_g": 1.0 + nrm(ks[1], (DEPTH, D_MODEL), 0.02),
        "w_in": nrm(ks[2], (DEPTH, D_MODEL, IN_COLS), D_MODEL ** -0.5),
        "fox_f_bias": 3.0 + nrm(ks[3], (DEPTH, FOX_HEADS), 0.5),
        "q_norm_g": 1.0 + nrm(ks[4], (DEPTH, FOX_HEAD_DIM), 0.02),
        "k_norm_g": 1.0 + nrm(ks[5], (DEPTH, FOX_HEAD_DIM), 0.02),
        "ml_i_bias": nrm(ks[6], (DEPTH, MLSTM_HEADS), 0.1),
        "ml_f_bias": 3.0 + nrm(ks[7], (DEPTH, MLSTM_HEADS), 0.5),
        "ml_out_norm_g": 1.0 + nrm(ks[8], (DEPTH, ML_V_W), 0.02),
        "w_branch_a": nrm(ks[9], (DEPTH, FOX_W, D_MODEL), FOX_W ** -0.5),
        "w_branch_b": nrm(ks[10], (DEPTH, ML_V_W, D_MODEL), ML_V_W ** -0.5),
        "w_out": nrm(ks[11], (DEPTH, D_MODEL, D_MODEL), D_MODEL ** -0.5),
        "norm2_g": 1.0 + nrm(ks[12], (DEPTH, D_MODEL), 0.02),
        "w_router": nrm(ks[13], (DEPTH, D_MODEL, N_EXPERTS), D_MODEL ** -0.5),
        "b_router": nrm(ks[14], (DEPTH, N_EXPERTS), 0.01),
        "w_gate": nrm(ks[15], (DEPTH, N_EXPERTS, D_MODEL, D_FF), D_MODEL ** -0.5),
        "b_gate": nrm(ks[16], (DEPTH, N_EXPERTS, D_FF), 0.02),
        "w_up": nrm(ks[17], (DEPTH, N_EXPERTS, D_MODEL, D_FF), D_MODEL ** -0.5),
        "b_up": nrm(ks[18], (DEPTH, N_EXPERTS, D_FF), 0.02),
        "w_down": nrm(ks[19], (DEPTH, N_EXPERTS, D_FF, D_MODEL), D_FF ** -0.5),
        "b_down": nrm(ks[20], (DEPTH, N_EXPERTS, D_MODEL), 0.02),
    }


def reference(x, norm1_g, w_in, fox_f_bias, q_norm_g, k_norm_g, ml_i_bias, ml_f_bias,
              ml_out_norm_g, w_branch_a, w_branch_b, w_out, norm2_g, w_router, b_router,
              w_gate, b_gate, w_up, b_up, w_down, b_down):
    f32 = jnp.float32
    out_dtype = x.dtype
    bsz, seq, _ = x.shape
    h = x.astype(f32)
    split_points = [int(p) for p in np.cumsum(IN_WIDTHS)[:-1]]
    for layer in range(DEPTH):
        xn = rms_norm(h, norm1_g[layer])
        proj = xn @ w_in[layer].astype(f32)
        (fq, fk, fv, ff, mq, mk, mv, mi, mf, mo, ga, gb) = jnp.split(proj, split_points, axis=-1)

        fq = rms_norm(fq.reshape(bsz, seq, FOX_HEADS, FOX_HEAD_DIM), q_norm_g[layer])
        fk = rms_norm(fk.reshape(bsz, seq, FOX_HEADS, FOX_HEAD_DIM), k_norm_g[layer])
        fv = fv.reshape(bsz, seq, FOX_HEADS, FOX_HEAD_DIM)
        fox_log_f = jax.nn.log_sigmoid(ff + fox_f_bias[layer].astype(f32))
        y_a = forgetting_attention(fq, fk, fv, fox_log_f)

        i_pre = IGATE_CAP * jnp.tanh((mi + ml_i_bias[layer].astype(f32)) / IGATE_CAP)
        ml_log_f = jax.nn.log_sigmoid(mf + ml_f_bias[layer].astype(f32))
        h_b = mlstm_chunkwise(mq.reshape(bsz, seq, MLSTM_HEADS, MLSTM_QK_DIM),
                              mk.reshape(bsz, seq, MLSTM_HEADS, MLSTM_QK_DIM),
                              mv.reshape(bsz, seq, MLSTM_HEADS, MLSTM_V_DIM),
                              i_pre, ml_log_f)
        h_b = h_b * lax.rsqrt(jnp.mean(h_b * h_b, axis=-1, keepdims=True) + EPS)
        y_b = h_b.reshape(bsz, seq, ML_V_W) * ml_out_norm_g[layer].astype(f32) * jax.nn.sigmoid(mo)

        merged = (jax.nn.sigmoid(ga) * (y_a @ w_branch_a[layer].astype(f32))
                  + jax.nn.sigmoid(gb) * (y_b @ w_branch_b[layer].astype(f32)))
        h = h + merged @ w_out[layer].astype(f32)

        t = rms_norm(h, norm2_g[layer])
        router_logits = t @ w_router[layer].astype(f32) + b_router[layer].astype(f32)
        top_vals, top_idx = lax.top_k(router_logits, TOP_K)
        top_w = jax.nn.softmax(top_vals, axis=-1)
        gates = jnp.sum(jax.nn.one_hot(top_idx, N_EXPERTS, dtype=f32) * top_w[..., None], axis=-2)
        moe = jnp.zeros_like(h)
        for e in range(N_EXPERTS):
            g = t @ w_gate[layer, e].astype(f32) + b_gate[layer, e].astype(f32)
            u = t @ w_up[layer, e].astype(f32) + b_up[layer, e].astype(f32)
            g = jnp.minimum(g, SWIGLU_LIMIT)
            u = jnp.clip(u, -SWIGLU_LIMIT, SWIGLU_LIMIT)
            act = (u + 1.0) * g * jax.nn.sigmoid(SWIGLU_ALPHA * g)
            y_e = act @ w_down[layer, e].astype(f32) + b_down[layer, e].astype(f32)
            moe = moe + gates[..., e:e + 1] * y_e
        h = h + moe
    return h.astype(out_dtype)
```

```python
import functools

import jax
import jax.numpy as jnp
from jax import lax
from jax.experimental import pallas as pl
from jax.experimental.pallas import tpu as pltpu

F32 = jnp.float32
BF16 = jnp.bfloat16

D_MODEL = 2048
FOX_HEADS = 8
FOX_HEAD_DIM = 128
FOX_W = FOX_HEADS * FOX_HEAD_DIM
ML_HEADS = 4
ML_QK_DIM = 128
ML_V_DIM = 256
ML_QK_W = ML_HEADS * ML_QK_DIM
ML_V_W = ML_HEADS * ML_V_DIM
IGATE_CAP = 15.0
N_EXPERTS = 32
TOP_K = 4
D_FF = 2048
SWIGLU_LIMIT = 7.0
SWIGLU_ALPHA = 1.702
EPS = 1e-5

IN_WIDTHS = (FOX_W, FOX_W, FOX_W, FOX_HEADS,
             ML_QK_W, ML_QK_W, ML_V_W, ML_HEADS, ML_HEADS, ML_V_W,
             D_MODEL, D_MODEL)

LANES = 128
VMEM_LIMIT = 56 * 1024 * 1024

COL_FQ, COL_FK, COL_FV = 0, FOX_W, 2 * FOX_W
COL_MQ = 3 * FOX_W
COL_MK = COL_MQ + ML_QK_W
COL_MV = COL_MK + ML_QK_W
COL_MO = COL_MV + ML_V_W
COL_GA = COL_MO + ML_V_W
COL_GB = COL_GA + D_MODEL
PROJ_W = COL_GB + D_MODEL
GCOL_FF, GCOL_MI, GCOL_MF = 0, FOX_HEADS, FOX_HEADS + ML_HEADS


def _cparams(n_axes, vmem=VMEM_LIMIT):
    return pltpu.CompilerParams(dimension_semantics=("arbitrary",) * n_axes,
                                vmem_limit_bytes=vmem)


def _log_sigmoid(x):
    return jnp.minimum(x, 0.0) - jnp.log1p(jnp.exp(-jnp.abs(x)))


def _in_proj_kernel(x_ref, g1_ref, w_ref, wg_ref, cs_ref, o_ref, gate_ref, xn_ref,
                    *, n_norm_blocks, row_chunk):
    j = pl.program_id(1)
    tm = x_ref.shape[0]
    tn = w_ref.shape[1]

    @pl.when(j == 0)
    def _():
        def body(c, carry):
            r0 = pl.multiple_of(c * row_chunk, row_chunk)
            x = x_ref[pl.ds(r0, row_chunk), :]
            ms = jnp.mean(x * x, axis=-1, keepdims=True)
            xn_ref[pl.ds(r0, row_chunk), :] = (x * lax.rsqrt(ms + EPS) * g1_ref[...]).astype(BF16)
            return carry
        lax.fori_loop(0, tm // row_chunk, body, 0)
        gate_ref[...] = jnp.dot(xn_ref[...], wg_ref[...], preferred_element_type=F32)

    acc = jnp.dot(xn_ref[...], w_ref[...], preferred_element_type=F32)
    cs = cs_ref[...]

    @pl.when(j < n_norm_blocks)
    def _():
        for s in range(tn // LANES):
            a = acc[:, s * LANES:(s + 1) * LANES]
            ms = jnp.mean(a * a, axis=-1, keepdims=True)
            y = a * lax.rsqrt(ms + EPS) * cs[:, s * LANES:(s + 1) * LANES]
            o_ref[:, s * LANES:(s + 1) * LANES] = y.astype(o_ref.dtype)

    @pl.when(j >= n_norm_blocks)
    def _():
        o_ref[...] = (acc * cs).astype(o_ref.dtype)


def _in_proj(x2d, g1, w_main, w_gate, colscale, *, tm, tn):
    t_rows = x2d.shape[0]
    grid = (t_rows // tm, PROJ_W // tn)
    kern = functools.partial(_in_proj_kernel, n_norm_blocks=(2 * FOX_W) // tn, row_chunk=128)
    return pl.pallas_call(
        kern,
        grid=grid,
        in_specs=[
            pl.BlockSpec((tm, D_MODEL), lambda i, j: (i, 0)),
            pl.BlockSpec((1, D_MODEL), lambda i, j: (0, 0)),
            pl.BlockSpec((D_MODEL, tn), lambda i, j: (0, j)),
            pl.BlockSpec((D_MODEL, LANES), lambda i, j: (0, 0)),
            pl.BlockSpec((1, tn), lambda i, j: (0, j)),
        ],
        out_specs=[
            pl.BlockSpec((tm, tn), lambda i, j: (i, j)),
            pl.BlockSpec((tm, LANES), lambda i, j: (i, 0)),
        ],
        out_shape=[
            jax.ShapeDtypeStruct((t_rows, PROJ_W), BF16),
            jax.ShapeDtypeStruct((t_rows, LANES), F32),
        ],
        scratch_shapes=[pltpu.VMEM((tm, D_MODEL), BF16)],
        compiler_params=_cparams(2),
        name="in_proj",
    )(x2d, g1, w_main, w_gate, colscale)


def _gate_prep_kernel(g_ref, b_ref, o_ref, carry_ref, *, chunk):
    s = pl.program_id(1)
    ts = g_ref.shape[1]

    @pl.when(s == 0)
    def _():
        carry_ref[...] = jnp.zeros_like(carry_ref)

    z = g_ref[0] + b_ref[...]
    log_f = _log_sigmoid(z)
    i_pre = IGATE_CAP * jnp.tanh(z / IGATE_CAP)
    row = lax.broadcasted_iota(jnp.int32, (ts, ts), 0)
    col = lax.broadcasted_iota(jnp.int32, (ts, ts), 1)
    tril = (col <= row)
    same_chunk = (row // chunk) == (col // chunk)
    tril_f = jnp.where(tril, 1.0, 0.0).astype(F32)
    tril_c = jnp.where(tril & same_chunk, 1.0, 0.0).astype(F32)
    run_sum = jnp.dot(tril_f, log_f, preferred_element_type=F32,
                      precision=lax.Precision.HIGHEST) + carry_ref[...]
    chunk_sum = jnp.dot(tril_c, log_f, preferred_element_type=F32,
                        precision=lax.Precision.HIGHEST)
    carry_ref[...] = run_sum[ts - 1:ts, :]
    lane = lax.broadcasted_iota(jnp.int32, (ts, LANES), 1)
    o_ref[0] = jnp.where(lane < GCOL_MI, run_sum, jnp.where(lane < GCOL_MF, i_pre, chunk_sum))


def _gate_prep(gates3d, bias, *, ts, chunk):
    bsz, seq, _ = gates3d.shape
    return pl.pallas_call(
        functools.partial(_gate_prep_kernel, chunk=chunk),
        grid=(bsz, seq // ts),
        in_specs=[
            pl.BlockSpec((1, ts, LANES), lambda b, s: (b, s, 0)),
            pl.BlockSpec((1, LANES), lambda b, s: (0, 0)),
        ],
        out_specs=pl.BlockSpec((1, ts, LANES), lambda b, s: (b, s, 0)),
        out_shape=jax.ShapeDtypeStruct((bsz, seq, LANES), F32),
        scratch_shapes=[pltpu.VMEM((1, LANES), F32)],
        compiler_params=_cparams(2),
        name="gate_prep",
    )(gates3d, bias)


def _fox_kernel(q_ref, k_ref, v_ref, ncf_ref, o_ref, *, tq):
    qi = pl.program_id(2)
    q = q_ref[0]
    hd = q.shape[-1]

    def block(kb, carry, masked):
        m, l, acc = carry
        off = pl.multiple_of(kb * tq, tq)
        k = k_ref[0, pl.ds(off, tq), :]
        v = v_ref[0, pl.ds(off, tq), :]
        s = lax.dot_general(q, k, (((1,), (1,)), ((), ())), preferred_element_type=F32)
        s = s + ncf_ref[0, :, pl.ds(off, tq)]
        if masked:
            row = lax.broadcasted_iota(jnp.int32, (tq, tq), 0)
            col = lax.broadcasted_iota(jnp.int32, (tq, tq), 1)
            s = jnp.where(row >= col, s, -jnp.inf)
        m_new = jnp.maximum(m, jnp.max(s, axis=-1, keepdims=True))
        alpha = jnp.exp(m - m_new)
        p = jnp.exp(s - m_new)
        l = alpha * l + jnp.sum(p, axis=-1, keepdims=True)
        acc = alpha * acc + jnp.dot(p.astype(BF16), v, preferred_element_type=F32)
        return m_new, l, acc

    init = (jnp.full((tq, 1), -jnp.inf, F32), jnp.zeros((tq, 1), F32), jnp.zeros((tq, hd), F32))
    carry = lax.fori_loop(0, qi, lambda kb, c: block(kb, c, False), init)
    _, l, acc = block(qi, carry, True)
    o_ref[0] = (acc / l).astype(o_ref.dtype)


def _fox_attention(proj3d, neg_cum_f, *, tq):
    bsz, seq, _ = proj3d.shape
    hd = FOX_HEAD_DIM
    return pl.pallas_call(
        functools.partial(_fox_kernel, tq=tq),
        grid=(bsz, FOX_HEADS, seq // tq),
        in_specs=[
            pl.BlockSpec((1, tq, hd), lambda b, h, i: (b, i, COL_FQ // hd + h)),
            pl.BlockSpec((1, seq, hd), lambda b, h, i: (b, 0, COL_FK // hd + h)),
            pl.BlockSpec((1, seq, hd), lambda b, h, i: (b, 0, COL_FV // hd + h)),
            pl.BlockSpec((1, 1, seq), lambda b, h, i: (b * FOX_HEADS + h, 0, 0)),
        ],
        out_specs=pl.BlockSpec((1, tq, hd), lambda b, h, i: (b, i, h)),
        out_shape=jax.ShapeDtypeStruct((bsz, seq, FOX_W), BF16),
        compiler_params=_cparams(3),
        name="fox_attn",
    )(proj3d, proj3d, proj3d, neg_cum_f)


def _mlstm_kernel(q_ref, k_ref, v_ref, mo_ref, gc_ref, gr_ref, gout_ref, o_ref,
                  c_ref, n_ref, m_ref):
    c_idx = pl.program_id(1)
    L = q_ref.shape[1]

    @pl.when(c_idx == 0)
    def _():
        c_ref[...] = jnp.zeros_like(c_ref)
        n_ref[...] = jnp.zeros_like(n_ref)
        m_ref[...] = jnp.zeros_like(m_ref)

    row = lax.broadcasted_iota(jnp.int32, (L, L), 0)
    col = lax.broadcasted_iota(jnp.int32, (L, L), 1)
    tril = col <= row

    for h in range(ML_HEADS):
        qs = slice(h * ML_QK_DIM, (h + 1) * ML_QK_DIM)
        vs = slice(h * ML_V_DIM, (h + 1) * ML_V_DIM)
        q = q_ref[0, :, qs]
        k = k_ref[0, :, qs]
        v = v_ref[0, :, vs]
        b_col = gc_ref[0, :, GCOL_MF + h:GCOL_MF + h + 1]
        i_col = gc_ref[0, :, GCOL_MI + h:GCOL_MI + h + 1]
        b_row = gr_ref[0, GCOL_MF + h:GCOL_MF + h + 1, :]
        i_row = gr_ref[0, GCOL_MI + h:GCOL_MI + h + 1, :]
        m_prev = m_ref[h, 0:1, 0:1]
        c_prev = c_ref[h]
        n_prev = n_ref[h]

        d_log = jnp.where(tril, b_col - b_row + i_row, -jnp.inf)
        g_inter = b_col + m_prev
        m_row = jnp.maximum(g_inter, jnp.max(d_log, axis=-1, keepdims=True))
        w_intra = jnp.exp(d_log - m_row)
        w_inter = jnp.exp(g_inter - m_row)
        qk = lax.dot_general(q, k, (((1,), (1,)), ((), ())), preferred_element_type=F32)
        scores = qk * w_intra
        num = (w_inter * jnp.dot(q, c_prev.astype(BF16), preferred_element_type=F32)
               + jnp.dot(scores.astype(BF16), v, preferred_element_type=F32))
        den = (w_inter * jnp.sum(q.astype(F32) * n_prev, axis=-1, keepdims=True)
               + jnp.sum(scores, axis=-1, keepdims=True))
        hh = num / jnp.maximum(jnp.abs(den), jnp.exp(-m_row))

        b_last = b_col[L - 1:L, :]
        a_log = b_last - b_col + i_col
        m_new = jnp.maximum(b_last + m_prev, jnp.max(a_log, axis=0, keepdims=True))
        decay = jnp.exp(b_last + m_prev - m_new)
        w_upd = jnp.exp(a_log - m_new)
        kw = k.astype(F32) * w_upd
        c_ref[h] = decay * c_prev + jnp.dot(kw.T.astype(BF16), v, preferred_element_type=F32)
        n_ref[h] = decay * n_prev + jnp.sum(kw, axis=0, keepdims=True)
        m_ref[h] = jnp.broadcast_to(m_new, m_ref.shape[1:])

        ms = jnp.mean(hh * hh, axis=-1, keepdims=True)
        y = (hh * lax.rsqrt(ms + EPS) * gout_ref[:, vs]
             * jax.nn.sigmoid(mo_ref[0, :, vs].astype(F32)))
        o_ref[0, :, vs] = y.astype(o_ref.dtype)


def _mlstm(proj3d, gate_cols, gate_rows, gout, *, chunk):
    bsz, seq, _ = proj3d.shape
    return pl.pallas_call(
        _mlstm_kernel,
        grid=(bsz, seq // chunk),
        in_specs=[
            pl.BlockSpec((1, chunk, ML_QK_W), lambda b, c: (b, c, COL_MQ // ML_QK_W)),
            pl.BlockSpec((1, chunk, ML_QK_W), lambda b, c: (b, c, COL_MK // ML_QK_W)),
            pl.BlockSpec((1, chunk, ML_V_W), lambda b, c: (b, c, COL_MV // ML_V_W)),
            pl.BlockSpec((1, chunk, ML_V_W), lambda b, c: (b, c, COL_MO // ML_V_W)),
            pl.BlockSpec((1, chunk, LANES), lambda b, c: (b, c, 0)),
            pl.BlockSpec((1, 16, chunk), lambda b, c: (b, 0, c)),
            pl.BlockSpec((1, ML_V_W), lambda b, c: (0, 0)),
        ],
        out_specs=pl.BlockSpec((1, chunk, ML_V_W), lambda b, c: (b, c, 0)),
        out_shape=jax.ShapeDtypeStruct((bsz, seq, ML_V_W), BF16),
        scratch_shapes=[
            pltpu.VMEM((ML_HEADS, ML_QK_DIM, ML_V_DIM), F32),
            pltpu.VMEM((ML_HEADS, 1, ML_QK_DIM), F32),
            pltpu.VMEM((ML_HEADS, 8, LANES), F32),
        ],
        compiler_params=_cparams(2),
        name="mlstm",
    )(proj3d, proj3d, proj3d, proj3d, gate_cols, gate_rows, gout)


def _merge_kernel(ya_ref, yb_ref, ga_ref, gb_ref, x_ref, wa_ref, wb_ref, wo_ref, g2_ref,
                  wr_ref, br_ref, h_ref, t_ref, route_ref, cnt_ref, carry_ref):
    i = pl.program_id(0)
    tm = x_ref.shape[0]

    @pl.when(i == 0)
    def _():
        carry_ref[...] = jnp.zeros_like(carry_ref)

    a = jnp.dot(ya_ref[...], wa_ref[...], preferred_element_type=F32)
    b = jnp.dot(yb_ref[...], wb_ref[...], preferred_element_type=F32)
    merged = (jax.nn.sigmoid(ga_ref[...].astype(F32)) * a
              + jax.nn.sigmoid(gb_ref[...].astype(F32)) * b)
    h = x_ref[...] + jnp.dot(merged.astype(BF16), wo_ref[...], preferred_element_type=F32)
    h_ref[...] = h
    ms = jnp.mean(h * h, axis=-1, keepdims=True)
    t = h * lax.rsqrt(ms + EPS) * g2_ref[...]
    t_ref[...] = t

    logits = jnp.dot(t, wr_ref[...], preferred_element_type=F32,
                     precision=lax.Precision.HIGHEST) + br_ref[...]
    lane = lax.broadcasted_iota(jnp.int32, (tm, LANES), 1).astype(F32)
    lg = logits
    sels, vals, idxs = [], [], []
    for _ in range(TOP_K):
        mx = jnp.max(lg, axis=-1, keepdims=True)
        ik = jnp.min(jnp.where(lg == mx, lane, float(LANES)), axis=-1, keepdims=True)
        sel = lane == ik
        sels.append(sel)
        vals.append(mx)
        idxs.append(ik)
        lg = jnp.where(sel, -jnp.inf, lg)
    exps = [jnp.exp(v - vals[0]) for v in vals]
    den = exps[0] + exps[1] + exps[2] + exps[3]
    mask = jnp.zeros((tm, LANES), F32)
    for sel in sels:
        mask = mask + jnp.where(sel, 1.0, 0.0)
    row = lax.broadcasted_iota(jnp.int32, (tm, tm), 0)
    col = lax.broadcasted_iota(jnp.int32, (tm, tm), 1)
    strict = jnp.where(col < row, 1.0, 0.0).astype(BF16)
    ranks = jnp.dot(strict, mask.astype(BF16), preferred_element_type=F32) + carry_ref[...]
    slab = jnp.zeros((tm, LANES), F32)
    for kk in range(TOP_K):
        rank_k = jnp.sum(jnp.where(sels[kk], ranks, 0.0), axis=-1, keepdims=True)
        slab = jnp.where(lane == float(kk), idxs[kk], slab)
        slab = jnp.where(lane == float(TOP_K + kk), rank_k, slab)
        slab = jnp.where(lane == float(2 * TOP_K + kk), exps[kk] / den, slab)
    route_ref[...] = slab
    new_carry = carry_ref[...] + jnp.sum(mask, axis=0, keepdims=True)
    carry_ref[...] = new_carry
    cnt_ref[...] = new_carry


def _merge(y_a, y_b, proj, x2d, w_a, w_b, w_o, g2, w_r, b_r, *, tm):
    t_rows = x2d.shape[0]
    const = lambda shape: pl.BlockSpec(shape, lambda i: (0, 0), pipeline_mode=pl.Buffered(1))
    return pl.pallas_call(
        _merge_kernel,
        grid=(t_rows // tm,),
        in_specs=[
            pl.BlockSpec((tm, FOX_W), lambda i: (i, 0)),
            pl.BlockSpec((tm, ML_V_W), lambda i: (i, 0)),
            pl.BlockSpec((tm, D_MODEL), lambda i: (i, COL_GA // D_MODEL)),
            pl.BlockSpec((tm, D_MODEL), lambda i: (i, COL_GB // D_MODEL)),
            pl.BlockSpec((tm, D_MODEL), lambda i: (i, 0)),
            const((FOX_W, D_MODEL)),
            const((ML_V_W, D_MODEL)),
            const((D_MODEL, D_MODEL)),
            const((1, D_MODEL)),
            const((D_MODEL, LANES)),
            const((1, LANES)),
        ],
        out_specs=[
            pl.BlockSpec((tm, D_MODEL), lambda i: (i, 0)),
            pl.BlockSpec((tm, D_MODEL), lambda i: (i, 0)),
            pl.BlockSpec((tm, LANES), lambda i: (i, 0)),
            pl.BlockSpec((1, LANES), lambda i: (0, 0)),
        ],
        out_shape=[
            jax.ShapeDtypeStruct((t_rows, D_MODEL), F32),
            jax.ShapeDtypeStruct((t_rows, D_MODEL), F32),
            jax.ShapeDtypeStruct((t_rows, LANES), F32),
            jax.ShapeDtypeStruct((1, LANES), F32),
        ],
        scratch_shapes=[pltpu.VMEM((1, LANES), F32)],
        compiler_params=_cparams(1),
        name="merge_router",
    )(y_a, y_b, proj, proj, x2d, w_a, w_b, w_o, g2, w_r, b_r)


def _dispatch_kernel(pos_ref, t_ref, xs_in_ref, xs_ref, sem):
    del xs_in_ref
    i = pl.program_id(0)
    tr = t_ref.shape[0]
    base = i * (tr * TOP_K)

    def row_copy(r, p):
        return pltpu.make_async_copy(t_ref.at[pl.ds(r, 1), :], xs_ref.at[pl.ds(p, 1), :], sem)

    def issue(r, carry):
        for kk in range(TOP_K):
            row_copy(r, pos_ref[base + r * TOP_K + kk]).start()
        return carry
    lax.fori_loop(0, tr, issue, 0)

    def drain(r, carry):
        for kk in range(TOP_K):
            row_copy(r, 0).wait()
        return carry
    lax.fori_loop(0, tr, drain, 0)


def _dispatch(pos_flat, t2d, xs_zero, *, tr):
    t_rows = t2d.shape[0]
    return pl.pallas_call(
        _dispatch_kernel,
        grid_spec=pltpu.PrefetchScalarGridSpec(
            num_scalar_prefetch=1,
            grid=(t_rows // tr,),
            in_specs=[
                pl.BlockSpec((tr, D_MODEL), lambda i, pos: (i, 0)),
                pl.BlockSpec(memory_space=pl.ANY),
            ],
            out_specs=pl.BlockSpec(memory_space=pl.ANY),
            scratch_shapes=[pltpu.SemaphoreType.DMA(())],
        ),
        out_shape=jax.ShapeDtypeStruct(xs_zero.shape, xs_zero.dtype),
        input_output_aliases={2: 0},
        compiler_params=_cparams(1),
        name="dispatch",
    )(pos_flat, t2d, xs_zero)


def _experts_kernel(te_ref, nu_ref, xs_ref, wg_ref, bg_ref, wu_ref, bu_ref, wd_ref, bd_ref,
                    o_ref, xb_ref):
    i = pl.program_id(0)
    j = pl.program_id(1)

    @pl.when(i < nu_ref[0])
    def _():
        @pl.when(j == 0)
        def _():
            xb_ref[...] = xs_ref[...].astype(BF16)

        xb = xb_ref[...]
        g = jnp.dot(xb, wg_ref[0].astype(BF16), preferred_element_type=F32) + bg_ref[0]
        u = jnp.dot(xb, wu_ref[0].astype(BF16), preferred_element_type=F32) + bu_ref[0]
        g = jnp.minimum(g, SWIGLU_LIMIT)
        u = jnp.clip(u, -SWIGLU_LIMIT, SWIGLU_LIMIT)
        act = (u + 1.0) * g * jax.nn.sigmoid(SWIGLU_ALPHA * g)
        y = jnp.dot(act.astype(BF16), wd_ref[0].astype(BF16), preferred_element_type=F32)

        @pl.when(j == 0)
        def _():
            o_ref[...] = y + bd_ref[0]

        @pl.when(j > 0)
        def _():
            o_ref[...] += y

    @pl.when((i >= nu_ref[0]) & (j == 0))
    def _():
        o_ref[...] = jnp.zeros_like(o_ref)


def _experts(tile_expert, n_used, xs, w_gate, b_gate, w_up, b_up, w_down, b_down, *, tm, tf):
    n_tiles = xs.shape[0] // tm
    n_ff = D_FF // tf

    def tile_i(i, te, nu):
        return jnp.minimum(i, nu[0] - 1)

    def ff_j(i, j, nu):
        return jnp.where(i < nu[0], j, n_ff - 1)

    return pl.pallas_call(
        _experts_kernel,
        grid_spec=pltpu.PrefetchScalarGridSpec(
            num_scalar_prefetch=2,
            grid=(n_tiles, n_ff),
            in_specs=[
                pl.BlockSpec((tm, D_MODEL), lambda i, j, te, nu: (tile_i(i, te, nu), 0)),
                pl.BlockSpec((1, D_MODEL, tf), lambda i, j, te, nu: (te[i], 0, ff_j(i, j, nu))),
                pl.BlockSpec((1, 1, tf), lambda i, j, te, nu: (te[i], 0, ff_j(i, j, nu))),
                pl.BlockSpec((1, D_MODEL, tf), lambda i, j, te, nu: (te[i], 0, ff_j(i, j, nu))),
                pl.BlockSpec((1, 1, tf), lambda i, j, te, nu: (te[i], 0, ff_j(i, j, nu))),
                pl.BlockSpec((1, tf, D_MODEL), lambda i, j, te, nu: (te[i], ff_j(i, j, nu), 0)),
                pl.BlockSpec((1, 1, D_MODEL), lambda i, j, te, nu: (te[i], 0, 0)),
            ],
            out_specs=pl.BlockSpec((tm, D_MODEL), lambda i, j, te, nu: (i, 0)),
            scratch_shapes=[pltpu.VMEM((tm, D_MODEL), BF16)],
        ),
        out_shape=jax.ShapeDtypeStruct(xs.shape, F32),
        compiler_params=_cparams(2),
        name="experts",
    )(tile_expert, n_used, xs, w_gate, b_gate, w_up, b_up, w_down, b_down)


def _combine_kernel(pos_ref, h_ref, w_ref, ys_ref, o_ref, buf_ref, sem):
    i = pl.program_id(0)
    n = pl.num_programs(0)
    tc = h_ref.shape[0]

    def row_copy(p, slot, kk, r):
        return pltpu.make_async_copy(ys_ref.at[pl.ds(p, 1), :],
                                     buf_ref.at[slot, kk, pl.ds(r, 1), :], sem.at[slot])

    def issue(step, slot):
        base = step * (tc * TOP_K)

        def body(r, carry):
            for kk in range(TOP_K):
                row_copy(pos_ref[base + r * TOP_K + kk], slot, kk, r).start()
            return carry
        lax.fori_loop(0, tc, body, 0)

    @pl.when(i == 0)
    def _():
        issue(0, 0)

    @pl.when(i + 1 < n)
    def _():
        issue(i + 1, (i + 1) % 2)

    slot = i % 2

    def drain(r, carry):
        for kk in range(TOP_K):
            row_copy(0, slot, kk, r).wait()
        return carry
    lax.fori_loop(0, tc, drain, 0)

    acc = h_ref[...]
    w = w_ref[...]
    for kk in range(TOP_K):
        acc = acc + w[:, kk:kk + 1] * buf_ref[slot, kk]
    o_ref[...] = acc


def _combine(pos_flat, h2d, w_top, ys, *, tc):
    t_rows = h2d.shape[0]
    return pl.pallas_call(
        _combine_kernel,
        grid_spec=pltpu.PrefetchScalarGridSpec(
            num_scalar_prefetch=1,
            grid=(t_rows // tc,),
            in_specs=[
                pl.BlockSpec((tc, D_MODEL), lambda i, pos: (i, 0)),
                pl.BlockSpec((tc, TOP_K), lambda i, pos: (i, 0)),
                pl.BlockSpec(memory_space=pl.ANY),
            ],
            out_specs=pl.BlockSpec((tc, D_MODEL), lambda i, pos: (i, 0)),
            scratch_shapes=[
                pltpu.VMEM((2, TOP_K, tc, D_MODEL), F32),
                pltpu.SemaphoreType.DMA((2,)),
            ],
        ),
        out_shape=jax.ShapeDtypeStruct(h2d.shape, F32),
        compiler_params=_cparams(1),
        name="combine",
    )(pos_flat, h2d, w_top, ys)


def _pick(n, pref):
    t = min(n, pref)
    assert n % t == 0, (n, t)
    return t


def kernel(x, norm1_g, w_in, fox_f_bias, q_norm_g, k_norm_g, ml_i_bias, ml_f_bias, ml_out_norm_g,
           w_branch_a, w_branch_b, w_out, norm2_g, w_router, b_router,
           w_gate, b_gate, w_up, b_up, w_down, b_down):
    bsz, seq, d_model = x.shape
    assert d_model == D_MODEL and norm1_g.shape[0] == 1, "single-layer block of width 2048"
    t_rows = bsz * seq
    x2d = x.reshape(t_rows, D_MODEL).astype(F32)

    offs = [0]
    for wdt in IN_WIDTHS:
        offs.append(offs[-1] + wdt)
    seg = lambda n: w_in[0][:, offs[n]:offs[n + 1]]
    w_main = jnp.concatenate([seg(0), seg(1), seg(2), seg(4), seg(5), seg(6), seg(9), seg(10), seg(11)],
                             axis=1).astype(BF16)
    w_gl = jnp.concatenate([seg(3), seg(7), seg(8),
                            jnp.zeros((D_MODEL, LANES - FOX_HEADS - 2 * ML_HEADS), F32)], axis=1).astype(BF16)
    colscale = jnp.ones((PROJ_W,), F32)
    colscale = colscale.at[COL_FQ:COL_FQ + FOX_W].set(
        jnp.tile(q_norm_g[0].astype(F32), FOX_HEADS) * (FOX_HEAD_DIM ** -0.5))
    colscale = colscale.at[COL_FK:COL_FK + FOX_W].set(jnp.tile(k_norm_g[0].astype(F32), FOX_HEADS))
    colscale = colscale.at[COL_MK:COL_MK + ML_QK_W].set(ML_QK_DIM ** -0.5)
    colscale = colscale.reshape(1, PROJ_W)
    gate_bias = jnp.concatenate([fox_f_bias[0], ml_i_bias[0], ml_f_bias[0],
                                 jnp.zeros((LANES - FOX_HEADS - 2 * ML_HEADS,), F32)]).astype(F32).reshape(1, LANES)

    proj, gates = _in_proj(x2d, norm1_g.astype(F32).reshape(1, D_MODEL), w_main, w_gl, colscale,
                           tm=_pick(t_rows, 1024), tn=512)
    chunk = _pick(seq, 256)
    gp = _gate_prep(gates.reshape(bsz, seq, LANES), gate_bias, ts=_pick(seq, 512), chunk=chunk)
    gp_rows = jnp.transpose(gp[:, :, :16], (0, 2, 1))
    neg_cum_f = (-gp_rows[:, GCOL_FF:GCOL_FF + FOX_HEADS, :]).reshape(bsz * FOX_HEADS, 1, seq)
    proj3d = proj.reshape(bsz, seq, PROJ_W)
    y_a = _fox_attention(proj3d, neg_cum_f, tq=_pick(seq, 512))
    y_b = _mlstm(proj3d, gp, gp_rows, ml_out_norm_g.astype(F32).reshape(1, ML_V_W), chunk=chunk)

    w_r = jnp.concatenate([w_router[0].astype(F32), jnp.zeros((D_MODEL, LANES - N_EXPERTS), F32)], axis=1)
    b_r = jnp.concatenate([b_router[0].astype(F32), jnp.full((LANES - N_EXPERTS,), -jnp.inf, F32)]).reshape(1, LANES)
    h2d, t2d, route, counts = _merge(
        y_a.reshape(t_rows, FOX_W), y_b.reshape(t_rows, ML_V_W), proj, x2d,
        w_branch_a[0].astype(BF16), w_branch_b[0].astype(BF16), w_out[0].astype(BF16),
        norm2_g.astype(F32).reshape(1, D_MODEL), w_r, b_r, tm=_pick(t_rows, 256))

    tm_e = _pick(t_rows * TOP_K, 512)
    n_tiles = (t_rows * TOP_K) // tm_e + N_EXPERTS
    top_idx = route[:, 0:TOP_K].astype(jnp.int32)
    top_rank = route[:, TOP_K:2 * TOP_K].astype(jnp.int32)
    top_w = route[:, 2 * TOP_K:3 * TOP_K]
    cnt = counts[0, :N_EXPERTS].astype(jnp.int32)
    tiles_per_e = (cnt + tm_e - 1) // tm_e
    tile_end = jnp.cumsum(tiles_per_e)
    row_start = (tile_end - tiles_per_e) * tm_e
    pos_flat = (row_start[top_idx] + top_rank).reshape(-1)
    n_used = tile_end[-1:]
    tile_ids = jnp.minimum(jnp.arange(n_tiles, dtype=jnp.int32), n_used[0] - 1)
    tile_expert = jnp.minimum(jnp.searchsorted(tile_end, tile_ids, side="right"),
                              N_EXPERTS - 1).astype(jnp.int32)

    xs = _dispatch(pos_flat, t2d, jnp.zeros((n_tiles * tm_e, D_MODEL), F32), tr=_pick(t_rows, 128))
    ys = _experts(tile_expert, n_used.astype(jnp.int32), xs,
                  w_gate[0], b_gate[0].reshape(N_EXPERTS, 1, D_FF),
                  w_up[0], b_up[0].reshape(N_EXPERTS, 1, D_FF),
                  w_down[0], b_down[0].reshape(N_EXPERTS, 1, D_MODEL), tm=tm_e, tf=256)
    out = _combine(pos_flat, h2d, top_w, ys, tc=_pick(t_rows, 64))
    return out.reshape(bsz, seq, D_MODEL).astype(x.dtype)
```

```python
import functools

import jax
import jax.numpy as jnp
from jax import lax
from jax.experimental import pallas as pl
from jax.experimental.pallas import tpu as pltpu

F32 = jnp.float32
BF16 = jnp.bfloat16

D_MODEL = 2048
FOX_HEADS = 8
FOX_HEAD_DIM = 128
FOX_W = FOX_HEADS * FOX_HEAD_DIM
ML_HEADS = 4
ML_QK_DIM = 128
ML_V_DIM = 256
ML_QK_W = ML_HEADS * ML_QK_DIM
ML_V_W = ML_HEADS * ML_V_DIM
IGATE_CAP = 15.0
N_EXPERTS = 32
TOP_K = 4
D_FF = 2048
SWIGLU_LIMIT = 7.0
SWIGLU_ALPHA = 1.702
EPS = 1e-5

IN_WIDTHS = (FOX_W, FOX_W, FOX_W, FOX_HEADS,
             ML_QK_W, ML_QK_W, ML_V_W, ML_HEADS, ML_HEADS, ML_V_W,
             D_MODEL, D_MODEL)

LANES = 128
VMEM_LIMIT = 56 * 1024 * 1024

COL_FQ, COL_FK, COL_FV = 0, FOX_W, 2 * FOX_W
COL_MQ = 3 * FOX_W
COL_MK = COL_MQ + ML_QK_W
COL_MV = COL_MK + ML_QK_W
COL_MO = COL_MV + ML_V_W
COL_GA = COL_MO + ML_V_W
COL_GB = COL_GA + D_MODEL
PROJ_W = COL_GB + D_MODEL
GCOL_FF, GCOL_MI, GCOL_MF = 0, FOX_HEADS, FOX_HEADS + ML_HEADS

HALF = D_MODEL // 2
LOG2_E = 1.4426950408889634


def _cparams(n_axes, vmem=VMEM_LIMIT):
    return pltpu.CompilerParams(dimension_semantics=("arbitrary",) * n_axes,
                                vmem_limit_bytes=vmem)


def _log_sigmoid(x):
    return jnp.minimum(x, 0.0) - jnp.log1p(jnp.exp(-jnp.abs(x)))


def _pack_rows(x):
    lo = lax.bitcast_convert_type(x[:, :HALF].astype(BF16).astype(F32), jnp.uint32)
    hi = lax.bitcast_convert_type(x[:, HALF:].astype(BF16).astype(F32), jnp.uint32)
    return hi | (lo >> 16)


def _unpack_rows(p):
    lo = lax.bitcast_convert_type(p << 16, F32)
    hi = lax.bitcast_convert_type(p & jnp.uint32(0xFFFF0000), F32)
    return lo, hi


def _rms_norm_rows(h, g):
    ms = jnp.mean(h * h, axis=-1, keepdims=True)
    return h * lax.rsqrt(ms + EPS) * g


def _in_proj_kernel(x_ref, g1_ref, w_ref, wg_ref, cs_ref, o_ref, gate_ref, xn_ref,
                    *, n_norm_blocks, row_chunk):
    j = pl.program_id(1)
    tm = x_ref.shape[0]
    tn = w_ref.shape[1]

    @pl.when(j == 0)
    def _():
        def body(c, carry):
            r0 = pl.multiple_of(c * row_chunk, row_chunk)
            x = x_ref[pl.ds(r0, row_chunk), :]
            ms = jnp.mean(x * x, axis=-1, keepdims=True)
            xn_ref[pl.ds(r0, row_chunk), :] = (x * lax.rsqrt(ms + EPS) * g1_ref[...]).astype(BF16)
            return carry
        lax.fori_loop(0, tm // row_chunk, body, 0)
        gate_ref[...] = jnp.dot(xn_ref[...], wg_ref[...], preferred_element_type=F32)

    acc = jnp.dot(xn_ref[...], w_ref[...], preferred_element_type=F32)
    cs = cs_ref[...]

    @pl.when(j < n_norm_blocks)
    def _():
        for s in range(tn // LANES):
            a = acc[:, s * LANES:(s + 1) * LANES]
            ms = jnp.mean(a * a, axis=-1, keepdims=True)
            y = a * lax.rsqrt(ms + EPS) * cs[:, s * LANES:(s + 1) * LANES]
            o_ref[:, s * LANES:(s + 1) * LANES] = y.astype(o_ref.dtype)

    @pl.when(j >= n_norm_blocks)
    def _():
        o_ref[...] = (acc * cs).astype(o_ref.dtype)


def _in_proj(x2d, g1, w_main, w_gate, colscale, *, tm, tn):
    t_rows = x2d.shape[0]
    grid = (t_rows // tm, PROJ_W // tn)
    kern = functools.partial(_in_proj_kernel, n_norm_blocks=(2 * FOX_W) // tn, row_chunk=128)
    return pl.pallas_call(
        kern,
        grid=grid,
        in_specs=[
            pl.BlockSpec((tm, D_MODEL), lambda i, j: (i, 0)),
            pl.BlockSpec((1, D_MODEL), lambda i, j: (0, 0)),
            pl.BlockSpec((D_MODEL, tn), lambda i, j: (0, j)),
            pl.BlockSpec((D_MODEL, LANES), lambda i, j: (0, 0)),
            pl.BlockSpec((1, tn), lambda i, j: (0, j)),
        ],
        out_specs=[
            pl.BlockSpec((tm, tn), lambda i, j: (i, j)),
            pl.BlockSpec((tm, LANES), lambda i, j: (i, 0)),
        ],
        out_shape=[
            jax.ShapeDtypeStruct((t_rows, PROJ_W), BF16),
            jax.ShapeDtypeStruct((t_rows, LANES), F32),
        ],
        scratch_shapes=[pltpu.VMEM((tm, D_MODEL), BF16)],
        compiler_params=_cparams(2),
        name="in_proj",
    )(x2d, g1, w_main, w_gate, colscale)


def _gate_prep_kernel(g_ref, b_ref, o_ref, carry_ref, *, chunk):
    s = pl.program_id(1)
    ts = g_ref.shape[1]

    @pl.when(s == 0)
    def _():
        carry_ref[...] = jnp.zeros_like(carry_ref)

    z = g_ref[0] + b_ref[...]
    log_f = _log_sigmoid(z)
    i_pre = IGATE_CAP * jnp.tanh(z / IGATE_CAP)
    row = lax.broadcasted_iota(jnp.int32, (ts, ts), 0)
    col = lax.broadcasted_iota(jnp.int32, (ts, ts), 1)
    tril = (col <= row)
    same_chunk = (row // chunk) == (col // chunk)
    tril_f = jnp.where(tril, 1.0, 0.0).astype(F32)
    tril_c = jnp.where(tril & same_chunk, 1.0, 0.0).astype(F32)
    run_sum = jnp.dot(tril_f, log_f, preferred_element_type=F32,
                      precision=lax.Precision.HIGHEST) + carry_ref[...]
    chunk_sum = jnp.dot(tril_c, log_f, preferred_element_type=F32,
                        precision=lax.Precision.HIGHEST)
    carry_ref[...] = run_sum[ts - 1:ts, :]
    lane = lax.broadcasted_iota(jnp.int32, (ts, LANES), 1)
    o_ref[0] = jnp.where(lane < GCOL_MI, run_sum, jnp.where(lane < GCOL_MF, i_pre, chunk_sum))


def _gate_prep(gates3d, bias, *, ts, chunk):
    bsz, seq, _ = gates3d.shape
    return pl.pallas_call(
        functools.partial(_gate_prep_kernel, chunk=chunk),
        grid=(bsz, seq // ts),
        in_specs=[
            pl.BlockSpec((1, ts, LANES), lambda b, s: (b, s, 0)),
            pl.BlockSpec((1, LANES), lambda b, s: (0, 0)),
        ],
        out_specs=pl.BlockSpec((1, ts, LANES), lambda b, s: (b, s, 0)),
        out_shape=jax.ShapeDtypeStruct((bsz, seq, LANES), F32),
        scratch_shapes=[pltpu.VMEM((1, LANES), F32)],
        compiler_params=_cparams(2),
        name="gate_prep",
    )(gates3d, bias)


def _fox_kernel(q_ref, k_ref, v_ref, ncf_ref, o_ref, *, tq):
    qi = pl.program_id(2)
    q = q_ref[0]
    hd = q.shape[-1]

    def block(kb, carry, masked):
        m, l, acc = carry
        off = pl.multiple_of(kb * tq, tq)
        k = k_ref[0, pl.ds(off, tq), :]
        v = v_ref[0, pl.ds(off, tq), :]
        s = lax.dot_general(q, k, (((1,), (1,)), ((), ())), preferred_element_type=F32)
        s = s + ncf_ref[0, :, pl.ds(off, tq)]
        if masked:
            row = lax.broadcasted_iota(jnp.int32, (tq, tq), 0)
            col = lax.broadcasted_iota(jnp.int32, (tq, tq), 1)
            s = jnp.where(row >= col, s, -jnp.inf)
        m_new = jnp.maximum(m, jnp.max(s, axis=-1, keepdims=True))
        alpha = jnp.exp2(m - m_new)
        p = jnp.exp2(s - m_new)
        l = alpha * l + jnp.sum(p, axis=-1, keepdims=True)
        acc = alpha * acc + jnp.dot(p.astype(BF16), v, preferred_element_type=F32)
        return m_new, l, acc

    init = (jnp.full((tq, 1), -jnp.inf, F32), jnp.zeros((tq, 1), F32), jnp.zeros((tq, hd), F32))
    carry = lax.fori_loop(0, qi, lambda kb, c: block(kb, c, False), init)
    _, l, acc = block(qi, carry, True)
    o_ref[0] = (acc / l).astype(o_ref.dtype)


def _fox_attention(proj3d, neg_cum_f, *, tq):
    bsz, seq, _ = proj3d.shape
    hd = FOX_HEAD_DIM
    return pl.pallas_call(
        functools.partial(_fox_kernel, tq=tq),
        grid=(bsz, FOX_HEADS, seq // tq),
        in_specs=[
            pl.BlockSpec((1, tq, hd), lambda b, h, i: (b, i, COL_FQ // hd + h)),
            pl.BlockSpec((1, seq, hd), lambda b, h, i: (b, 0, COL_FK // hd + h)),
            pl.BlockSpec((1, seq, hd), lambda b, h, i: (b, 0, COL_FV // hd + h)),
            pl.BlockSpec((1, 1, seq), lambda b, h, i: (b * FOX_HEADS + h, 0, 0)),
        ],
        out_specs=pl.BlockSpec((1, tq, hd), lambda b, h, i: (b, i, h)),
        out_shape=jax.ShapeDtypeStruct((bsz, seq, FOX_W), BF16),
        compiler_params=_cparams(3),
        name="fox_attn",
    )(proj3d, proj3d, proj3d, neg_cum_f)


def _mlstm_kernel(q_ref, k_ref, v_ref, mo_ref, gc_ref, gr_ref, gout_ref, o_ref,
                  c_ref, n_ref, m_ref):
    c_idx = pl.program_id(1)
    L = q_ref.shape[1]

    @pl.when(c_idx == 0)
    def _():
        c_ref[...] = jnp.zeros_like(c_ref)
        n_ref[...] = jnp.zeros_like(n_ref)
        m_ref[...] = jnp.zeros_like(m_ref)

    row = lax.broadcasted_iota(jnp.int32, (L, L), 0)
    col = lax.broadcasted_iota(jnp.int32, (L, L), 1)
    tril = col <= row

    for h in range(ML_HEADS):
        qs = slice(h * ML_QK_DIM, (h + 1) * ML_QK_DIM)
        vs = slice(h * ML_V_DIM, (h + 1) * ML_V_DIM)
        q = q_ref[0, :, qs]
        k = k_ref[0, :, qs]
        v = v_ref[0, :, vs]
        b_col = gc_ref[0, :, GCOL_MF + h:GCOL_MF + h + 1]
        i_col = gc_ref[0, :, GCOL_MI + h:GCOL_MI + h + 1]
        b_row = gr_ref[0, GCOL_MF + h:GCOL_MF + h + 1, :]
        i_row = gr_ref[0, GCOL_MI + h:GCOL_MI + h + 1, :]
        m_prev = m_ref[h, 0:1, 0:1]
        c_prev = c_ref[h]
        n_prev = n_ref[h]

        d_log = jnp.where(tril, b_col - b_row + i_row, -jnp.inf)
        g_inter = b_col + m_prev
        m_row = jnp.maximum(g_inter, jnp.max(d_log, axis=-1, keepdims=True))
        w_intra = jnp.exp(d_log - m_row)
        w_inter = jnp.exp(g_inter - m_row)
        qk = lax.dot_general(q, k, (((1,), (1,)), ((), ())), preferred_element_type=F32)
        scores = qk * w_intra
        num = (w_inter * jnp.dot(q, c_prev.astype(BF16), preferred_element_type=F32)
               + jnp.dot(scores.astype(BF16), v, preferred_element_type=F32))
        den = (w_inter * jnp.sum(q.astype(F32) * n_prev, axis=-1, keepdims=True)
               + jnp.sum(scores, axis=-1, keepdims=True))
        hh = num / jnp.maximum(jnp.abs(den), jnp.exp(-m_row))

        b_last = b_col[L - 1:L, :]
        a_log = b_last - b_col + i_col
        m_new = jnp.maximum(b_last + m_prev, jnp.max(a_log, axis=0, keepdims=True))
        decay = jnp.exp(b_last + m_prev - m_new)
        w_upd = jnp.exp(a_log - m_new)
        kw = k.astype(F32) * w_upd
        c_ref[h] = decay * c_prev + jnp.dot(kw.T.astype(BF16), v, preferred_element_type=F32)
        n_ref[h] = decay * n_prev + jnp.sum(kw, axis=0, keepdims=True)
        m_ref[h] = jnp.broadcast_to(m_new, m_ref.shape[1:])

        ms = jnp.mean(hh * hh, axis=-1, keepdims=True)
        y = (hh * lax.rsqrt(ms + EPS) * gout_ref[:, vs]
             * jax.nn.sigmoid(mo_ref[0, :, vs].astype(F32)))
        o_ref[0, :, vs] = y.astype(o_ref.dtype)


def _mlstm(proj3d, gate_cols, gate_rows, gout, *, chunk):
    bsz, seq, _ = proj3d.shape
    return pl.pallas_call(
        _mlstm_kernel,
        grid=(bsz, seq // chunk),
        in_specs=[
            pl.BlockSpec((1, chunk, ML_QK_W), lambda b, c: (b, c, COL_MQ // ML_QK_W)),
            pl.BlockSpec((1, chunk, ML_QK_W), lambda b, c: (b, c, COL_MK // ML_QK_W)),
            pl.BlockSpec((1, chunk, ML_V_W), lambda b, c: (b, c, COL_MV // ML_V_W)),
            pl.BlockSpec((1, chunk, ML_V_W), lambda b, c: (b, c, COL_MO // ML_V_W)),
            pl.BlockSpec((1, chunk, LANES), lambda b, c: (b, c, 0)),
            pl.BlockSpec((1, 16, chunk), lambda b, c: (b, 0, c)),
            pl.BlockSpec((1, ML_V_W), lambda b, c: (0, 0)),
        ],
        out_specs=pl.BlockSpec((1, chunk, ML_V_W), lambda b, c: (b, c, 0)),
        out_shape=jax.ShapeDtypeStruct((bsz, seq, ML_V_W), BF16),
        scratch_shapes=[
            pltpu.VMEM((ML_HEADS, ML_QK_DIM, ML_V_DIM), F32),
            pltpu.VMEM((ML_HEADS, 1, ML_QK_DIM), F32),
            pltpu.VMEM((ML_HEADS, 8, LANES), F32),
        ],
        compiler_params=_cparams(2),
        name="mlstm",
    )(proj3d, proj3d, proj3d, proj3d, gate_cols, gate_rows, gout)


def _merge_kernel(ya_ref, yb_ref, ga_ref, gb_ref, x_ref, wa_ref, wb_ref, wo_ref, g2_ref,
                  wr_ref, br_ref, h_ref, route_ref, cnt_ref, carry_ref):
    i = pl.program_id(0)
    tm = x_ref.shape[0]

    @pl.when(i == 0)
    def _():
        carry_ref[...] = jnp.zeros_like(carry_ref)

    a = jnp.dot(ya_ref[...], wa_ref[...], preferred_element_type=F32)
    b = jnp.dot(yb_ref[...], wb_ref[...], preferred_element_type=F32)
    merged = (jax.nn.sigmoid(ga_ref[...].astype(F32)) * a
              + jax.nn.sigmoid(gb_ref[...].astype(F32)) * b)
    h = x_ref[...] + jnp.dot(merged.astype(BF16), wo_ref[...], preferred_element_type=F32)
    h_ref[...] = h
    t = _rms_norm_rows(h, g2_ref[...])

    logits = jnp.dot(t, wr_ref[...], preferred_element_type=F32,
                     precision=lax.Precision.HIGHEST) + br_ref[...]
    lane = lax.broadcasted_iota(jnp.int32, (tm, LANES), 1).astype(F32)
    lg = logits
    sels, vals, idxs = [], [], []
    for _ in range(TOP_K):
        mx = jnp.max(lg, axis=-1, keepdims=True)
        ik = jnp.min(jnp.where(lg == mx, lane, float(LANES)), axis=-1, keepdims=True)
        sel = lane == ik
        sels.append(sel)
        vals.append(mx)
        idxs.append(ik)
        lg = jnp.where(sel, -jnp.inf, lg)
    exps = [jnp.exp(v - vals[0]) for v in vals]
    den = exps[0] + exps[1] + exps[2] + exps[3]
    mask = jnp.zeros((tm, LANES), F32)
    for sel in sels:
        mask = mask + jnp.where(sel, 1.0, 0.0)
    row = lax.broadcasted_iota(jnp.int32, (tm, tm), 0)
    col = lax.broadcasted_iota(jnp.int32, (tm, tm), 1)
    strict = jnp.where(col < row, 1.0, 0.0).astype(BF16)
    ranks = jnp.dot(strict, mask.astype(BF16), preferred_element_type=F32) + carry_ref[...]
    slab = jnp.zeros((tm, LANES), F32)
    for kk in range(TOP_K):
        rank_k = jnp.sum(jnp.where(sels[kk], ranks, 0.0), axis=-1, keepdims=True)
        slab = jnp.where(lane == float(kk), idxs[kk], slab)
        slab = jnp.where(lane == float(TOP_K + kk), rank_k, slab)
        slab = jnp.where(lane == float(2 * TOP_K + kk), exps[kk] / den, slab)
    route_ref[...] = slab
    new_carry = carry_ref[...] + jnp.sum(mask, axis=0, keepdims=True)
    carry_ref[...] = new_carry
    cnt_ref[...] = new_carry


def _merge(y_a, y_b, proj, x2d, w_a, w_b, w_o, g2, w_r, b_r, *, tm):
    t_rows = x2d.shape[0]
    const = lambda shape: pl.BlockSpec(shape, lambda i: (0, 0), pipeline_mode=pl.Buffered(1))
    return pl.pallas_call(
        _merge_kernel,
        grid=(t_rows // tm,),
        in_specs=[
            pl.BlockSpec((tm, FOX_W), lambda i: (i, 0)),
            pl.BlockSpec((tm, ML_V_W), lambda i: (i, 0)),
            pl.BlockSpec((tm, D_MODEL), lambda i: (i, COL_GA // D_MODEL)),
            pl.BlockSpec((tm, D_MODEL), lambda i: (i, COL_GB // D_MODEL)),
            pl.BlockSpec((tm, D_MODEL), lambda i: (i, 0)),
            const((FOX_W, D_MODEL)),
            const((ML_V_W, D_MODEL)),
            const((D_MODEL, D_MODEL)),
            const((1, D_MODEL)),
            const((D_MODEL, LANES)),
            const((1, LANES)),
        ],
        out_specs=[
            pl.BlockSpec((tm, D_MODEL), lambda i: (i, 0)),
            pl.BlockSpec((tm, LANES), lambda i: (i, 0)),
            pl.BlockSpec((1, LANES), lambda i: (0, 0)),
        ],
        out_shape=[
            jax.ShapeDtypeStruct((t_rows, D_MODEL), F32),
            jax.ShapeDtypeStruct((t_rows, LANES), F32),
            jax.ShapeDtypeStruct((1, LANES), F32),
        ],
        scratch_shapes=[pltpu.VMEM((1, LANES), F32)],
        compiler_params=_cparams(1),
        name="merge_router",
    )(y_a, y_b, proj, proj, x2d, w_a, w_b, w_o, g2, w_r, b_r)


def _dispatch_kernel(pos_ref, h_ref, g2_ref, xs_in_ref, xs_ref, buf_ref, sem):
    del xs_in_ref
    i = pl.program_id(0)
    n = pl.num_programs(0)
    tr = h_ref.shape[0]
    slot = i % 2
    buf_ref[slot] = _pack_rows(_rms_norm_rows(h_ref[...], g2_ref[...]))

    def row_copy(s, r, p):
        return pltpu.make_async_copy(buf_ref.at[s, pl.ds(r, 1), :], xs_ref.at[pl.ds(p, 1), :], sem.at[s])

    base = i * (tr * TOP_K)

    def issue(r, carry):
        for kk in range(TOP_K):
            row_copy(slot, r, pos_ref[base + r * TOP_K + kk]).start()
        return carry
    lax.fori_loop(0, tr, issue, 0)

    def drain(s):
        def body(r, carry):
            for kk in range(TOP_K):
                row_copy(s, r, 0).wait()
            return carry
        lax.fori_loop(0, tr, body, 0)

    @pl.when(i > 0)
    def _():
        drain(1 - slot)

    @pl.when(i == n - 1)
    def _():
        drain(slot)


def _dispatch(pos_flat, h2d, g2, xs_zero, *, tr):
    t_rows = h2d.shape[0]
    return pl.pallas_call(
        _dispatch_kernel,
        grid_spec=pltpu.PrefetchScalarGridSpec(
            num_scalar_prefetch=1,
            grid=(t_rows // tr,),
            in_specs=[
                pl.BlockSpec((tr, D_MODEL), lambda i, pos: (i, 0)),
                pl.BlockSpec((1, D_MODEL), lambda i, pos: (0, 0)),
                pl.BlockSpec(memory_space=pl.ANY),
            ],
            out_specs=pl.BlockSpec(memory_space=pl.ANY),
            scratch_shapes=[
                pltpu.VMEM((2, tr, HALF), jnp.uint32),
                pltpu.SemaphoreType.DMA((2,)),
            ],
        ),
        out_shape=jax.ShapeDtypeStruct(xs_zero.shape, xs_zero.dtype),
        input_output_aliases={3: 0},
        compiler_params=_cparams(1),
        name="dispatch",
    )(pos_flat, h2d, g2, xs_zero)


def _experts_kernel(te_ref, nv_ref, nu_ref, xs_ref, wg_ref, bg_ref, wu_ref, bu_ref, wd_ref, bd_ref,
                    o_ref, xb_ref, acc_ref, wgb_ref, wub_ref, wdb_ref, *, sub):
    del te_ref, nu_ref
    i = pl.program_id(0)
    j = pl.program_id(1)
    n_ff = pl.num_programs(1)
    tm = xs_ref.shape[0]
    valid = nv_ref[i]

    @pl.when(valid > 0)
    def _():
        wgb_ref[...] = wg_ref[0].astype(BF16)
        wub_ref[...] = wu_ref[0].astype(BF16)
        wdb_ref[...] = wd_ref[0].astype(BF16)

    for s in range(tm // sub):
        rows = pl.ds(s * sub, sub)

        @pl.when(s * sub < valid)
        def _():
            @pl.when(j == 0)
            def _():
                lo, hi = _unpack_rows(xs_ref[rows, :])
                xb_ref[rows, :HALF] = lo.astype(BF16)
                xb_ref[rows, HALF:] = hi.astype(BF16)

            xb = xb_ref[rows, :]
            g = jnp.dot(xb, wgb_ref[...], preferred_element_type=F32) + bg_ref[0]
            u = jnp.dot(xb, wub_ref[...], preferred_element_type=F32) + bu_ref[0]
            g = jnp.minimum(g, SWIGLU_LIMIT)
            u = jnp.clip(u, -SWIGLU_LIMIT, SWIGLU_LIMIT)
            act = (u + 1.0) * g * jax.nn.sigmoid(SWIGLU_ALPHA * g)
            y = jnp.dot(act.astype(BF16), wdb_ref[...], preferred_element_type=F32)

            @pl.when(j == 0)
            def _():
                acc_ref[rows, :] = y + bd_ref[0]

            @pl.when(j > 0)
            def _():
                acc_ref[rows, :] += y

        @pl.when((s * sub >= valid) & (j == 0))
        def _():
            acc_ref[rows, :] = jnp.zeros((sub, D_MODEL), F32)

    @pl.when(j == n_ff - 1)
    def _():
        o_ref[...] = _pack_rows(acc_ref[...])


def _experts(tile_expert, tile_valid, n_used, xs, w_gate, b_gate, w_up, b_up, w_down, b_down,
             *, tm, tf, sub):
    n_tiles = xs.shape[0] // tm
    n_ff = D_FF // tf

    def tile_i(i, nu):
        return jnp.minimum(i, nu[0] - 1)

    def ff_j(i, j, nu):
        return jnp.where(i < nu[0], j, n_ff - 1)

    return pl.pallas_call(
        functools.partial(_experts_kernel, sub=sub),
        grid_spec=pltpu.PrefetchScalarGridSpec(
            num_scalar_prefetch=3,
            grid=(n_tiles, n_ff),
            in_specs=[
                pl.BlockSpec((tm, HALF), lambda i, j, te, nv, nu: (tile_i(i, nu), 0)),
                pl.BlockSpec((1, D_MODEL, tf), lambda i, j, te, nv, nu: (te[i], 0, ff_j(i, j, nu))),
                pl.BlockSpec((1, 1, tf), lambda i, j, te, nv, nu: (te[i], 0, ff_j(i, j, nu))),
                pl.BlockSpec((1, D_MODEL, tf), lambda i, j, te, nv, nu: (te[i], 0, ff_j(i, j, nu))),
                pl.BlockSpec((1, 1, tf), lambda i, j, te, nv, nu: (te[i], 0, ff_j(i, j, nu))),
                pl.BlockSpec((1, tf, D_MODEL), lambda i, j, te, nv, nu: (te[i], ff_j(i, j, nu), 0)),
                pl.BlockSpec((1, 1, D_MODEL), lambda i, j, te, nv, nu: (te[i], 0, 0)),
            ],
            out_specs=pl.BlockSpec((tm, HALF), lambda i, j, te, nv, nu: (i, 0)),
            scratch_shapes=[
                pltpu.VMEM((tm, D_MODEL), BF16),
                pltpu.VMEM((tm, D_MODEL), F32),
                pltpu.VMEM((D_MODEL, tf), BF16),
                pltpu.VMEM((D_MODEL, tf), BF16),
                pltpu.VMEM((tf, D_MODEL), BF16),
            ],
        ),
        out_shape=jax.ShapeDtypeStruct(xs.shape, jnp.uint32),
        compiler_params=_cparams(2),
        name="experts",
    )(tile_expert, tile_valid, n_used, xs, w_gate, b_gate, w_up, b_up, w_down, b_down)


def _combine_kernel(pos_ref, h_ref, w_ref, ys_ref, o_ref, buf_ref, sem):
    i = pl.program_id(0)
    n = pl.num_programs(0)
    tc = h_ref.shape[0]

    def row_copy(p, slot, kk, r):
        return pltpu.make_async_copy(ys_ref.at[pl.ds(p, 1), :],
                                     buf_ref.at[slot, kk, pl.ds(r, 1), :], sem.at[slot])

    def issue(step, slot):
        base = step * (tc * TOP_K)

        def body(r, carry):
            for kk in range(TOP_K):
                row_copy(pos_ref[base + r * TOP_K + kk], slot, kk, r).start()
            return carry
        lax.fori_loop(0, tc, body, 0)

    @pl.when(i == 0)
    def _():
        issue(0, 0)

    @pl.when(i + 1 < n)
    def _():
        issue(i + 1, (i + 1) % 2)

    slot = i % 2

    def drain(r, carry):
        for kk in range(TOP_K):
            row_copy(0, slot, kk, r).wait()
        return carry
    lax.fori_loop(0, tc, drain, 0)

    w = w_ref[...]
    acc_lo = h_ref[:, :HALF]
    acc_hi = h_ref[:, HALF:]
    for kk in range(TOP_K):
        lo, hi = _unpack_rows(buf_ref[slot, kk])
        acc_lo = acc_lo + w[:, kk:kk + 1] * lo
        acc_hi = acc_hi + w[:, kk:kk + 1] * hi
    o_ref[:, :HALF] = acc_lo
    o_ref[:, HALF:] = acc_hi


def _combine(pos_flat, h2d, w_top, ys, *, tc):
    t_rows = h2d.shape[0]
    return pl.pallas_call(
        _combine_kernel,
        grid_spec=pltpu.PrefetchScalarGridSpec(
            num_scalar_prefetch=1,
            grid=(t_rows // tc,),
            in_specs=[
                pl.BlockSpec((tc, D_MODEL), lambda i, pos: (i, 0)),
                pl.BlockSpec((tc, TOP_K), lambda i, pos: (i, 0)),
                pl.BlockSpec(memory_space=pl.ANY),
            ],
            out_specs=pl.BlockSpec((tc, D_MODEL), lambda i, pos: (i, 0)),
            scratch_shapes=[
                pltpu.VMEM((2, TOP_K, tc, HALF), jnp.uint32),
                pltpu.SemaphoreType.DMA((2,)),
            ],
        ),
        out_shape=jax.ShapeDtypeStruct(h2d.shape, F32),
        compiler_params=_cparams(1),
        name="combine",
    )(pos_flat, h2d, w_top, ys)


def _pick(n, pref):
    t = min(n, pref)
    assert n % t == 0, (n, t)
    return t


def kernel(x, norm1_g, w_in, fox_f_bias, q_norm_g, k_norm_g, ml_i_bias, ml_f_bias, ml_out_norm_g,
           w_branch_a, w_branch_b, w_out, norm2_g, w_router, b_router,
           w_gate, b_gate, w_up, b_up, w_down, b_down):
    bsz, seq, d_model = x.shape
    assert d_model == D_MODEL and norm1_g.shape[0] == 1, "single-layer block of width 2048"
    t_rows = bsz * seq
    x2d = x.reshape(t_rows, D_MODEL).astype(F32)

    offs = [0]
    for wdt in IN_WIDTHS:
        offs.append(offs[-1] + wdt)
    seg = lambda n: w_in[0][:, offs[n]:offs[n + 1]]
    w_main = jnp.concatenate([seg(0), seg(1), seg(2), seg(4), seg(5), seg(6), seg(9), seg(10), seg(11)],
                             axis=1).astype(BF16)
    w_gl = jnp.concatenate([seg(3), seg(7), seg(8),
                            jnp.zeros((D_MODEL, LANES - FOX_HEADS - 2 * ML_HEADS), F32)], axis=1).astype(BF16)
    colscale = jnp.ones((PROJ_W,), F32)
    colscale = colscale.at[COL_FQ:COL_FQ + FOX_W].set(
        jnp.tile(q_norm_g[0].astype(F32), FOX_HEADS) * (FOX_HEAD_DIM ** -0.5 * LOG2_E))
    colscale = colscale.at[COL_FK:COL_FK + FOX_W].set(jnp.tile(k_norm_g[0].astype(F32), FOX_HEADS))
    colscale = colscale.at[COL_MK:COL_MK + ML_QK_W].set(ML_QK_DIM ** -0.5)
    colscale = colscale.reshape(1, PROJ_W)
    gate_bias = jnp.concatenate([fox_f_bias[0], ml_i_bias[0], ml_f_bias[0],
                                 jnp.zeros((LANES - FOX_HEADS - 2 * ML_HEADS,), F32)]).astype(F32).reshape(1, LANES)
    g2 = norm2_g.astype(F32).reshape(1, D_MODEL)

    proj, gates = _in_proj(x2d, norm1_g.astype(F32).reshape(1, D_MODEL), w_main, w_gl, colscale,
                           tm=_pick(t_rows, 1024), tn=512)
    chunk = _pick(seq, 256)
    gp = _gate_prep(gates.reshape(bsz, seq, LANES), gate_bias, ts=_pick(seq, 512), chunk=chunk)
    gp_rows = jnp.transpose(gp[:, :, :16], (0, 2, 1))
    neg_cum_f = (-LOG2_E * gp_rows[:, GCOL_FF:GCOL_FF + FOX_HEADS, :]).reshape(bsz * FOX_HEADS, 1, seq)
    proj3d = proj.reshape(bsz, seq, PROJ_W)
    y_a = _fox_attention(proj3d, neg_cum_f, tq=_pick(seq, 512))
    y_b = _mlstm(proj3d, gp, gp_rows, ml_out_norm_g.astype(F32).reshape(1, ML_V_W), chunk=chunk)

    w_r = jnp.concatenate([w_router[0].astype(F32), jnp.zeros((D_MODEL, LANES - N_EXPERTS), F32)], axis=1)
    b_r = jnp.concatenate([b_router[0].astype(F32), jnp.full((LANES - N_EXPERTS,), -jnp.inf, F32)]).reshape(1, LANES)
    h2d, route, counts = _merge(
        y_a.reshape(t_rows, FOX_W), y_b.reshape(t_rows, ML_V_W), proj, x2d,
        w_branch_a[0].astype(BF16), w_branch_b[0].astype(BF16), w_out[0].astype(BF16),
        g2, w_r, b_r, tm=_pick(t_rows, 512))

    tm_e = _pick(t_rows * TOP_K, 1024)
    n_tiles = (t_rows * TOP_K) // tm_e + N_EXPERTS
    top_idx = route[:, 0:TOP_K].astype(jnp.int32)
    top_rank = route[:, TOP_K:2 * TOP_K].astype(jnp.int32)
    top_w = route[:, 2 * TOP_K:3 * TOP_K]
    cnt = counts[0, :N_EXPERTS].astype(jnp.int32)
    tiles_per_e = (cnt + tm_e - 1) // tm_e
    tile_end = jnp.cumsum(tiles_per_e)
    tile_begin = tile_end - tiles_per_e
    pos_flat = ((tile_begin * tm_e)[top_idx] + top_rank).reshape(-1)
    n_used = tile_end[-1:].astype(jnp.int32)
    tile_ids = jnp.arange(n_tiles, dtype=jnp.int32)
    used_ids = jnp.minimum(tile_ids, n_used[0] - 1)
    tile_expert = jnp.sum((used_ids[:, None] >= tile_end[None, :]).astype(jnp.int32), axis=1)
    tile_valid = jnp.clip(cnt[tile_expert] - (used_ids - tile_begin[tile_expert]) * tm_e, 0, tm_e)
    tile_valid = jnp.where(tile_ids < n_used[0], tile_valid, 0).astype(jnp.int32)

    xs = _dispatch(pos_flat, h2d, g2, jnp.zeros((n_tiles * tm_e, HALF), jnp.uint32), tr=_pick(t_rows, 128))
    ys = _experts(tile_expert, tile_valid, n_used, xs,
                  w_gate[0], b_gate[0].reshape(N_EXPERTS, 1, D_FF),
                  w_up[0], b_up[0].reshape(N_EXPERTS, 1, D_FF),
                  w_down[0], b_down[0].reshape(N_EXPERTS, 1, D_MODEL),
                  tm=tm_e, tf=256, sub=_pick(tm_e, 512))
    out = _combine(pos_flat, h2d, top_w, ys, tc=_pick(t_rows, 128))
    return out.reshape(bsz, seq, D_MODEL).astype(x.dtype)
```

```python
import functools

import jax
import jax.numpy as jnp
from jax import lax
from jax.experimental import pallas as pl
from jax.experimental.pallas import tpu as pltpu

F32 = jnp.float32
BF16 = jnp.bfloat16

D_MODEL = 2048
FOX_HEADS = 8
FOX_HEAD_DIM = 128
FOX_W = FOX_HEADS * FOX_HEAD_DIM
ML_HEADS = 4
ML_QK_DIM = 128
ML_V_DIM = 256
ML_QK_W = ML_HEADS * ML_QK_DIM
ML_V_W = ML_HEADS * ML_V_DIM
IGATE_CAP = 15.0
N_EXPERTS = 32
TOP_K = 4
D_FF = 2048
SWIGLU_LIMIT = 7.0
SWIGLU_ALPHA = 1.702
EPS = 1e-5

IN_WIDTHS = (FOX_W, FOX_W, FOX_W, FOX_HEADS,
             ML_QK_W, ML_QK_W, ML_V_W, ML_HEADS, ML_HEADS, ML_V_W,
             D_MODEL, D_MODEL)

LANES = 128
VMEM_LIMIT = 56 * 1024 * 1024

COL_FQ, COL_FK, COL_FV = 0, FOX_W, 2 * FOX_W
COL_MQ = 3 * FOX_W
COL_MK = COL_MQ + ML_QK_W
COL_MV = COL_MK + ML_QK_W
COL_MO = COL_MV + ML_V_W
COL_GA = COL_MO + ML_V_W
COL_GB = COL_GA + D_MODEL
PROJ_W = COL_GB + D_MODEL
GCOL_FF, GCOL_MI, GCOL_MF = 0, FOX_HEADS, FOX_HEADS + ML_HEADS

HALF = D_MODEL // 2
LOG2_E = 1.4426950408889634


def _cparams(n_axes, vmem=VMEM_LIMIT):
    return pltpu.CompilerParams(dimension_semantics=("arbitrary",) * n_axes,
                                vmem_limit_bytes=vmem)


def _log_sigmoid(x):
    return jnp.minimum(x, 0.0) - jnp.log1p(jnp.exp(-jnp.abs(x)))


def _pack_rows(x):
    lo = lax.bitcast_convert_type(x[:, :HALF].astype(BF16).astype(F32), jnp.uint32)
    hi = lax.bitcast_convert_type(x[:, HALF:].astype(BF16).astype(F32), jnp.uint32)
    return hi | (lo >> 16)


def _unpack_rows(p):
    lo = lax.bitcast_convert_type(p << 16, F32)
    hi = lax.bitcast_convert_type(p & jnp.uint32(0xFFFF0000), F32)
    return lo, hi


def _rms_norm_rows(h, g):
    ms = jnp.mean(h * h, axis=-1, keepdims=True)
    return h * lax.rsqrt(ms + EPS) * g


def _in_proj_kernel(x_ref, g1_ref, w_ref, wg_ref, cs_ref, o_ref, gate_ref, xn_ref,
                    *, n_norm_blocks, row_chunk):
    j = pl.program_id(1)
    tm = x_ref.shape[0]
    tn = w_ref.shape[1]

    @pl.when(j == 0)
    def _():
        def body(c, carry):
            r0 = pl.multiple_of(c * row_chunk, row_chunk)
            x = x_ref[pl.ds(r0, row_chunk), :]
            ms = jnp.mean(x * x, axis=-1, keepdims=True)
            xn_ref[pl.ds(r0, row_chunk), :] = (x * lax.rsqrt(ms + EPS) * g1_ref[...]).astype(BF16)
            return carry
        lax.fori_loop(0, tm // row_chunk, body, 0)
        gate_ref[...] = jnp.dot(xn_ref[...], wg_ref[...], preferred_element_type=F32)

    acc = jnp.dot(xn_ref[...], w_ref[...], preferred_element_type=F32)
    cs = cs_ref[...]

    @pl.when(j < n_norm_blocks)
    def _():
        for s in range(tn // LANES):
            a = acc[:, s * LANES:(s + 1) * LANES]
            ms = jnp.mean(a * a, axis=-1, keepdims=True)
            y = a * lax.rsqrt(ms + EPS) * cs[:, s * LANES:(s + 1) * LANES]
            o_ref[:, s * LANES:(s + 1) * LANES] = y.astype(o_ref.dtype)

    @pl.when(j >= n_norm_blocks)
    def _():
        o_ref[...] = (acc * cs).astype(o_ref.dtype)


def _in_proj(x2d, g1, w_main, w_gate, colscale, *, tm, tn):
    t_rows = x2d.shape[0]
    grid = (t_rows // tm, PROJ_W // tn)
    kern = functools.partial(_in_proj_kernel, n_norm_blocks=(2 * FOX_W) // tn, row_chunk=128)
    return pl.pallas_call(
        kern,
        grid=grid,
        in_specs=[
            pl.BlockSpec((tm, D_MODEL), lambda i, j: (i, 0)),
            pl.BlockSpec((1, D_MODEL), lambda i, j: (0, 0)),
            pl.BlockSpec((D_MODEL, tn), lambda i, j: (0, j)),
            pl.BlockSpec((D_MODEL, LANES), lambda i, j: (0, 0)),
            pl.BlockSpec((1, tn), lambda i, j: (0, j)),
        ],
        out_specs=[
            pl.BlockSpec((tm, tn), lambda i, j: (i, j)),
            pl.BlockSpec((tm, LANES), lambda i, j: (i, 0)),
        ],
        out_shape=[
            jax.ShapeDtypeStruct((t_rows, PROJ_W), BF16),
            jax.ShapeDtypeStruct((t_rows, LANES), F32),
        ],
        scratch_shapes=[pltpu.VMEM((tm, D_MODEL), BF16)],
        compiler_params=_cparams(2),
        name="in_proj",
    )(x2d, g1, w_main, w_gate, colscale)


def _gate_prep_kernel(g_ref, b_ref, o_ref, carry_ref, *, chunk):
    s = pl.program_id(1)
    ts = g_ref.shape[1]

    @pl.when(s == 0)
    def _():
        carry_ref[...] = jnp.zeros_like(carry_ref)

    z = g_ref[0] + b_ref[...]
    log_f = _log_sigmoid(z)
    i_pre = IGATE_CAP * jnp.tanh(z / IGATE_CAP)
    row = lax.broadcasted_iota(jnp.int32, (ts, ts), 0)
    col = lax.broadcasted_iota(jnp.int32, (ts, ts), 1)
    tril = (col <= row)
    same_chunk = (row // chunk) == (col // chunk)
    tril_f = jnp.where(tril, 1.0, 0.0).astype(F32)
    tril_c = jnp.where(tril & same_chunk, 1.0, 0.0).astype(F32)
    run_sum = jnp.dot(tril_f, log_f, preferred_element_type=F32,
                      precision=lax.Precision.HIGHEST) + carry_ref[...]
    chunk_sum = jnp.dot(tril_c, log_f, preferred_element_type=F32,
                        precision=lax.Precision.HIGHEST)
    carry_ref[...] = run_sum[ts - 1:ts, :]
    lane = lax.broadcasted_iota(jnp.int32, (ts, LANES), 1)
    o_ref[0] = jnp.where(lane < GCOL_MI, run_sum, jnp.where(lane < GCOL_MF, i_pre, chunk_sum))


def _gate_prep(gates3d, bias, *, ts, chunk):
    bsz, seq, _ = gates3d.shape
    return pl.pallas_call(
        functools.partial(_gate_prep_kernel, chunk=chunk),
        grid=(bsz, seq // ts),
        in_specs=[
            pl.BlockSpec((1, ts, LANES), lambda b, s: (b, s, 0)),
            pl.BlockSpec((1, LANES), lambda b, s: (0, 0)),
        ],
        out_specs=pl.BlockSpec((1, ts, LANES), lambda b, s: (b, s, 0)),
        out_shape=jax.ShapeDtypeStruct((bsz, seq, LANES), F32),
        scratch_shapes=[pltpu.VMEM((1, LANES), F32)],
        compiler_params=_cparams(2),
        name="gate_prep",
    )(gates3d, bias)


def _fox_kernel(q_ref, k_ref, v_ref, ncf_ref, o_ref, *, tq):
    qi = pl.program_id(2)
    hd = FOX_HEAD_DIM
    heads = q_ref.shape[-1] // hd

    def block(kb, carry, masked):
        off = pl.multiple_of(kb * tq, tq)
        new = []
        for hh in range(heads):
            cols = slice(hh * hd, (hh + 1) * hd)
            m, l, acc = carry[hh]
            q = q_ref[0, :, cols]
            k = k_ref[0, pl.ds(off, tq), cols]
            v = v_ref[0, pl.ds(off, tq), cols]
            s = lax.dot_general(q, k, (((1,), (1,)), ((), ())), preferred_element_type=F32)
            s = s + ncf_ref[0, hh:hh + 1, pl.ds(off, tq)]
            if masked:
                row = lax.broadcasted_iota(jnp.int32, (tq, tq), 0)
                col = lax.broadcasted_iota(jnp.int32, (tq, tq), 1)
                s = jnp.where(row >= col, s, -jnp.inf)
            m_new = jnp.maximum(m, jnp.max(s, axis=-1, keepdims=True))
            alpha = jnp.exp2(m - m_new)
            p = jnp.exp2(s - m_new)
            l = alpha * l + jnp.sum(p, axis=-1, keepdims=True)
            acc = alpha * acc + jnp.dot(p.astype(BF16), v, preferred_element_type=F32)
            new.append((m_new, l, acc))
        return tuple(new)

    init = tuple((jnp.full((tq, 1), -jnp.inf, F32), jnp.zeros((tq, 1), F32), jnp.zeros((tq, hd), F32))
                 for _ in range(heads))
    carry = lax.fori_loop(0, qi, lambda kb, c: block(kb, c, False), init)
    final = block(qi, carry, True)
    for hh in range(heads):
        _, l, acc = final[hh]
        o_ref[0, :, hh * hd:(hh + 1) * hd] = (acc / l).astype(o_ref.dtype)


def _fox_attention(proj3d, neg_cum_f, *, tq, heads):
    bsz, seq, _ = proj3d.shape
    hw = heads * FOX_HEAD_DIM
    groups = FOX_HEADS // heads
    return pl.pallas_call(
        functools.partial(_fox_kernel, tq=tq),
        grid=(bsz, groups, seq // tq),
        in_specs=[
            pl.BlockSpec((1, tq, hw), lambda b, g, i: (b, i, COL_FQ // hw + g)),
            pl.BlockSpec((1, seq, hw), lambda b, g, i: (b, 0, COL_FK // hw + g)),
            pl.BlockSpec((1, seq, hw), lambda b, g, i: (b, 0, COL_FV // hw + g)),
            pl.BlockSpec((1, heads, seq), lambda b, g, i: (b * groups + g, 0, 0)),
        ],
        out_specs=pl.BlockSpec((1, tq, hw), lambda b, g, i: (b, i, g)),
        out_shape=jax.ShapeDtypeStruct((bsz, seq, FOX_W), BF16),
        compiler_params=_cparams(3),
        name="fox_attn",
    )(proj3d, proj3d, proj3d, neg_cum_f)


def _mlstm_kernel(q_ref, k_ref, v_ref, mo_ref, gc_ref, gr_ref, gout_ref, o_ref,
                  c_ref, n_ref, m_ref):
    c_idx = pl.program_id(1)
    L = q_ref.shape[1]

    @pl.when(c_idx == 0)
    def _():
        c_ref[...] = jnp.zeros_like(c_ref)
        n_ref[...] = jnp.zeros_like(n_ref)
        m_ref[...] = jnp.zeros_like(m_ref)

    row = lax.broadcasted_iota(jnp.int32, (L, L), 0)
    col = lax.broadcasted_iota(jnp.int32, (L, L), 1)
    tril = col <= row

    for h in range(ML_HEADS):
        qs = slice(h * ML_QK_DIM, (h + 1) * ML_QK_DIM)
        vs = slice(h * ML_V_DIM, (h + 1) * ML_V_DIM)
        q = q_ref[0, :, qs]
        k = k_ref[0, :, qs]
        v = v_ref[0, :, vs]
        b_col = gc_ref[0, :, GCOL_MF + h:GCOL_MF + h + 1]
        i_col = gc_ref[0, :, GCOL_MI + h:GCOL_MI + h + 1]
        b_row = gr_ref[0, GCOL_MF + h:GCOL_MF + h + 1, :]
        i_row = gr_ref[0, GCOL_MI + h:GCOL_MI + h + 1, :]
        m_prev = m_ref[h, 0:1, 0:1]
        c_prev = c_ref[h]
        n_prev = n_ref[h]

        d_log = jnp.where(tril, b_col - b_row + i_row, -jnp.inf)
        g_inter = b_col + m_prev
        m_row = jnp.maximum(g_inter, jnp.max(d_log, axis=-1, keepdims=True))
        w_intra = jnp.exp(d_log - m_row)
        w_inter = jnp.exp(g_inter - m_row)
        qk = lax.dot_general(q, k, (((1,), (1,)), ((), ())), preferred_element_type=F32)
        scores = qk * w_intra
        num = (w_inter * jnp.dot(q, c_prev.astype(BF16), preferred_element_type=F32)
               + jnp.dot(scores.astype(BF16), v, preferred_element_type=F32))
        den = (w_inter * jnp.sum(q.astype(F32) * n_prev, axis=-1, keepdims=True)
               + jnp.sum(scores, axis=-1, keepdims=True))
        hh = num / jnp.maximum(jnp.abs(den), jnp.exp(-m_row))

        b_last = b_col[L - 1:L, :]
        a_log = b_last - b_col + i_col
        m_new = jnp.maximum(b_last + m_prev, jnp.max(a_log, axis=0, keepdims=True))
        decay = jnp.exp(b_last + m_prev - m_new)
        w_upd = jnp.exp(a_log - m_new)
        kw = k.astype(F32) * w_upd
        c_ref[h] = decay * c_prev + jnp.dot(kw.T.astype(BF16), v, preferred_element_type=F32)
        n_ref[h] = decay * n_prev + jnp.sum(kw, axis=0, keepdims=True)
        m_ref[h] = jnp.broadcast_to(m_new, m_ref.shape[1:])

        ms = jnp.mean(hh * hh, axis=-1, keepdims=True)
        y = (hh * lax.rsqrt(ms + EPS) * gout_ref[:, vs]
             * jax.nn.sigmoid(mo_ref[0, :, vs].astype(F32)))
        o_ref[0, :, vs] = y.astype(o_ref.dtype)


def _mlstm(proj3d, gate_cols, gate_rows, gout, *, chunk):
    bsz, seq, _ = proj3d.shape
    return pl.pallas_call(
        _mlstm_kernel,
        grid=(bsz, seq // chunk),
        in_specs=[
            pl.BlockSpec((1, chunk, ML_QK_W), lambda b, c: (b, c, COL_MQ // ML_QK_W)),
            pl.BlockSpec((1, chunk, ML_QK_W), lambda b, c: (b, c, COL_MK // ML_QK_W)),
            pl.BlockSpec((1, chunk, ML_V_W), lambda b, c: (b, c, COL_MV // ML_V_W)),
            pl.BlockSpec((1, chunk, ML_V_W), lambda b, c: (b, c, COL_MO // ML_V_W)),
            pl.BlockSpec((1, chunk, LANES), lambda b, c: (b, c, 0)),
            pl.BlockSpec((1, 16, chunk), lambda b, c: (b, 0, c)),
            pl.BlockSpec((1, ML_V_W), lambda b, c: (0, 0)),
        ],
        out_specs=pl.BlockSpec((1, chunk, ML_V_W), lambda b, c: (b, c, 0)),
        out_shape=jax.ShapeDtypeStruct((bsz, seq, ML_V_W), BF16),
        scratch_shapes=[
            pltpu.VMEM((ML_HEADS, ML_QK_DIM, ML_V_DIM), F32),
            pltpu.VMEM((ML_HEADS, 1, ML_QK_DIM), F32),
            pltpu.VMEM((ML_HEADS, 8, LANES), F32),
        ],
        compiler_params=_cparams(2),
        name="mlstm",
    )(proj3d, proj3d, proj3d, proj3d, gate_cols, gate_rows, gout)


def _merge_kernel(ya_ref, yb_ref, ga_ref, gb_ref, x_ref, wa_ref, wb_ref, wo_ref, g2_ref,
                  wr_ref, br_ref, h_ref, route_ref, cnt_ref, carry_ref):
    i = pl.program_id(0)
    tm = x_ref.shape[0]

    @pl.when(i == 0)
    def _():
        carry_ref[...] = jnp.zeros_like(carry_ref)

    a = jnp.dot(ya_ref[...], wa_ref[...], preferred_element_type=F32)
    b = jnp.dot(yb_ref[...], wb_ref[...], preferred_element_type=F32)
    merged = (jax.nn.sigmoid(ga_ref[...].astype(F32)) * a
              + jax.nn.sigmoid(gb_ref[...].astype(F32)) * b)
    h = x_ref[...] + jnp.dot(merged.astype(BF16), wo_ref[...], preferred_element_type=F32)
    h_ref[...] = h
    t = _rms_norm_rows(h, g2_ref[...])

    logits = jnp.dot(t, wr_ref[...], preferred_element_type=F32,
                     precision=lax.Precision.HIGHEST) + br_ref[...]
    lane = lax.broadcasted_iota(jnp.int32, (tm, LANES), 1).astype(F32)
    lg = logits
    sels, vals, idxs = [], [], []
    for _ in range(TOP_K):
        mx = jnp.max(lg, axis=-1, keepdims=True)
        ik = jnp.min(jnp.where(lg == mx, lane, float(LANES)), axis=-1, keepdims=True)
        sel = lane == ik
        sels.append(sel)
        vals.append(mx)
        idxs.append(ik)
        lg = jnp.where(sel, -jnp.inf, lg)
    exps = [jnp.exp(v - vals[0]) for v in vals]
    den = exps[0] + exps[1] + exps[2] + exps[3]
    mask = jnp.zeros((tm, LANES), F32)
    for sel in sels:
        mask = mask + jnp.where(sel, 1.0, 0.0)
    row = lax.broadcasted_iota(jnp.int32, (tm, tm), 0)
    col = lax.broadcasted_iota(jnp.int32, (tm, tm), 1)
    strict = jnp.where(col < row, 1.0, 0.0).astype(BF16)
    ranks = jnp.dot(strict, mask.astype(BF16), preferred_element_type=F32) + carry_ref[...]
    slab = jnp.zeros((tm, LANES), F32)
    for kk in range(TOP_K):
        rank_k = jnp.sum(jnp.where(sels[kk], ranks, 0.0), axis=-1, keepdims=True)
        slab = jnp.where(lane == float(kk), idxs[kk], slab)
        slab = jnp.where(lane == float(TOP_K + kk), rank_k, slab)
        slab = jnp.where(lane == float(2 * TOP_K + kk), exps[kk] / den, slab)
    route_ref[...] = slab
    new_carry = carry_ref[...] + jnp.sum(mask, axis=0, keepdims=True)
    carry_ref[...] = new_carry
    cnt_ref[...] = new_carry


def _merge(y_a, y_b, proj, x2d, w_a, w_b, w_o, g2, w_r, b_r, *, tm):
    t_rows = x2d.shape[0]
    const = lambda shape: pl.BlockSpec(shape, lambda i: (0, 0), pipeline_mode=pl.Buffered(1))
    return pl.pallas_call(
        _merge_kernel,
        grid=(t_rows // tm,),
        in_specs=[
            pl.BlockSpec((tm, FOX_W), lambda i: (i, 0)),
            pl.BlockSpec((tm, ML_V_W), lambda i: (i, 0)),
            pl.BlockSpec((tm, D_MODEL), lambda i: (i, COL_GA // D_MODEL)),
            pl.BlockSpec((tm, D_MODEL), lambda i: (i, COL_GB // D_MODEL)),
            pl.BlockSpec((tm, D_MODEL), lambda i: (i, 0)),
            const((FOX_W, D_MODEL)),
            const((ML_V_W, D_MODEL)),
            const((D_MODEL, D_MODEL)),
            const((1, D_MODEL)),
            const((D_MODEL, LANES)),
            const((1, LANES)),
        ],
        out_specs=[
            pl.BlockSpec((tm, D_MODEL), lambda i: (i, 0)),
            pl.BlockSpec((tm, LANES), lambda i: (i, 0)),
            pl.BlockSpec((1, LANES), lambda i: (0, 0)),
        ],
        out_shape=[
            jax.ShapeDtypeStruct((t_rows, D_MODEL), F32),
            jax.ShapeDtypeStruct((t_rows, LANES), F32),
            jax.ShapeDtypeStruct((1, LANES), F32),
        ],
        scratch_shapes=[pltpu.VMEM((1, LANES), F32)],
        compiler_params=_cparams(1),
        name="merge_router",
    )(y_a, y_b, proj, proj, x2d, w_a, w_b, w_o, g2, w_r, b_r)


def _dispatch_kernel(pos_ref, h_ref, g2_ref, xs_in_ref, xs_ref, buf_ref, sem):
    del xs_in_ref
    i = pl.program_id(0)
    n = pl.num_programs(0)
    tr = h_ref.shape[0]
    slot = i % 2
    buf_ref[slot] = _pack_rows(_rms_norm_rows(h_ref[...], g2_ref[...]))

    def row_copy(s, r, p):
        return pltpu.make_async_copy(buf_ref.at[s, pl.ds(r, 1), :], xs_ref.at[pl.ds(p, 1), :], sem.at[s])

    base = i * (tr * TOP_K)

    def issue(r, carry):
        for kk in range(TOP_K):
            row_copy(slot, r, pos_ref[base + r * TOP_K + kk]).start()
        return carry
    lax.fori_loop(0, tr, issue, 0)

    def drain(s):
        def body(r, carry):
            for kk in range(TOP_K):
                row_copy(s, r, 0).wait()
            return carry
        lax.fori_loop(0, tr, body, 0)

    @pl.when(i > 0)
    def _():
        drain(1 - slot)

    @pl.when(i == n - 1)
    def _():
        drain(slot)


def _dispatch(pos_flat, h2d, g2, xs_zero, *, tr):
    t_rows = h2d.shape[0]
    return pl.pallas_call(
        _dispatch_kernel,
        grid_spec=pltpu.PrefetchScalarGridSpec(
            num_scalar_prefetch=1,
            grid=(t_rows // tr,),
            in_specs=[
                pl.BlockSpec((tr, D_MODEL), lambda i, pos: (i, 0)),
                pl.BlockSpec((1, D_MODEL), lambda i, pos: (0, 0)),
                pl.BlockSpec(memory_space=pl.ANY),
            ],
            out_specs=pl.BlockSpec(memory_space=pl.ANY),
            scratch_shapes=[
                pltpu.VMEM((2, tr, HALF), jnp.uint32),
                pltpu.SemaphoreType.DMA((2,)),
            ],
        ),
        out_shape=jax.ShapeDtypeStruct(xs_zero.shape, xs_zero.dtype),
        input_output_aliases={3: 0},
        compiler_params=_cparams(1),
        name="dispatch",
    )(pos_flat, h2d, g2, xs_zero)


def _experts_kernel(te_ref, nv_ref, nu_ref, xs_ref, wg_ref, bg_ref, wu_ref, bu_ref, wd_ref, bd_ref,
                    o_ref, xb_ref, acc_ref, *, sub):
    del te_ref, nu_ref
    i = pl.program_id(0)
    j = pl.program_id(1)
    n_ff = pl.num_programs(1)
    tm = xs_ref.shape[0]
    valid = nv_ref[i]

    @pl.when(j == 0)
    def _():
        lo, hi = _unpack_rows(xs_ref[...])
        xb_ref[:, :HALF] = lo.astype(BF16)
        xb_ref[:, HALF:] = hi.astype(BF16)
        acc_ref[...] = jnp.broadcast_to(bd_ref[0], acc_ref.shape)

    def ffn_chunk(n_rows):
        xb = xb_ref[0:n_rows, :]
        g = jnp.dot(xb, wg_ref[0].astype(BF16), preferred_element_type=F32) + bg_ref[0]
        u = jnp.dot(xb, wu_ref[0].astype(BF16), preferred_element_type=F32) + bu_ref[0]
        g = jnp.minimum(g, SWIGLU_LIMIT)
        u = jnp.clip(u, -SWIGLU_LIMIT, SWIGLU_LIMIT)
        act = (u + 1.0) * g * jax.nn.sigmoid(SWIGLU_ALPHA * g)
        acc_ref[0:n_rows, :] += jnp.dot(act.astype(BF16), wd_ref[0].astype(BF16),
                                        preferred_element_type=F32)

    n_sub = tm // sub
    for s in range(n_sub):
        lo_rows, hi_rows = s * sub, (s + 1) * sub
        upper = (valid <= hi_rows) if s + 1 < n_sub else True

        @pl.when((valid > lo_rows) & upper)
        def _():
            ffn_chunk(hi_rows)

    @pl.when(j == n_ff - 1)
    def _():
        o_ref[...] = _pack_rows(acc_ref[...])


def _experts(tile_expert, tile_valid, n_used, xs, w_gate, b_gate, w_up, b_up, w_down, b_down,
             *, tm, tf, sub):
    n_tiles = xs.shape[0] // tm
    n_ff = D_FF // tf

    def tile_i(i, nu):
        return jnp.maximum(jnp.minimum(i, nu[0] - 1), 0)

    def ff_j(i, j, nu):
        return jnp.where(i < nu[0], j, n_ff - 1)

    return pl.pallas_call(
        functools.partial(_experts_kernel, sub=sub),
        grid_spec=pltpu.PrefetchScalarGridSpec(
            num_scalar_prefetch=3,
            grid=(n_tiles, n_ff),
            in_specs=[
                pl.BlockSpec((tm, HALF), lambda i, j, te, nv, nu: (tile_i(i, nu), 0)),
                pl.BlockSpec((1, D_MODEL, tf), lambda i, j, te, nv, nu: (te[i], 0, ff_j(i, j, nu))),
                pl.BlockSpec((1, 1, tf), lambda i, j, te, nv, nu: (te[i], 0, ff_j(i, j, nu))),
                pl.BlockSpec((1, D_MODEL, tf), lambda i, j, te, nv, nu: (te[i], 0, ff_j(i, j, nu))),
                pl.BlockSpec((1, 1, tf), lambda i, j, te, nv, nu: (te[i], 0, ff_j(i, j, nu))),
                pl.BlockSpec((1, tf, D_MODEL), lambda i, j, te, nv, nu: (te[i], ff_j(i, j, nu), 0)),
                pl.BlockSpec((1, 1, D_MODEL), lambda i, j, te, nv, nu: (te[i], 0, 0)),
            ],
            out_specs=pl.BlockSpec((tm, HALF), lambda i, j, te, nv, nu: (i, 0)),
            scratch_shapes=[
                pltpu.VMEM((tm, D_MODEL), BF16),
                pltpu.VMEM((tm, D_MODEL), F32),
            ],
        ),
        out_shape=jax.ShapeDtypeStruct(xs.shape, jnp.uint32),
        compiler_params=_cparams(2),
        name="experts",
    )(tile_expert, tile_valid, n_used, xs, w_gate, b_gate, w_up, b_up, w_down, b_down)


def _combine_kernel(pos_ref, h_ref, w_ref, ys_ref, o_ref, buf_ref, sem):
    i = pl.program_id(0)
    n = pl.num_programs(0)
    tc = h_ref.shape[0]

    def row_copy(p, slot, kk, r):
        return pltpu.make_async_copy(ys_ref.at[pl.ds(p, 1), :],
                                     buf_ref.at[slot, kk, pl.ds(r, 1), :], sem.at[slot])

    def issue(step, slot):
        base = step * (tc * TOP_K)

        def body(r, carry):
            for kk in range(TOP_K):
                row_copy(pos_ref[base + r * TOP_K + kk], slot, kk, r).start()
            return carry
        lax.fori_loop(0, tc, body, 0)

    @pl.when(i == 0)
    def _():
        issue(0, 0)

    @pl.when(i + 1 < n)
    def _():
        issue(i + 1, (i + 1) % 2)

    slot = i % 2

    def drain(r, carry):
        for kk in range(TOP_K):
            row_copy(0, slot, kk, r).wait()
        return carry
    lax.fori_loop(0, tc, drain, 0)

    w = w_ref[...]
    acc_lo = h_ref[:, :HALF]
    acc_hi = h_ref[:, HALF:]
    for kk in range(TOP_K):
        lo, hi = _unpack_rows(buf_ref[slot, kk])
        acc_lo = acc_lo + w[:, kk:kk + 1] * lo
        acc_hi = acc_hi + w[:, kk:kk + 1] * hi
    o_ref[:, :HALF] = acc_lo
    o_ref[:, HALF:] = acc_hi


def _combine(pos_flat, h2d, w_top, ys, *, tc):
    t_rows = h2d.shape[0]
    return pl.pallas_call(
        _combine_kernel,
        grid_spec=pltpu.PrefetchScalarGridSpec(
            num_scalar_prefetch=1,
            grid=(t_rows // tc,),
            in_specs=[
                pl.BlockSpec((tc, D_MODEL), lambda i, pos: (i, 0)),
                pl.BlockSpec((tc, TOP_K), lambda i, pos: (i, 0)),
                pl.BlockSpec(memory_space=pl.ANY),
            ],
            out_specs=pl.BlockSpec((tc, D_MODEL), lambda i, pos: (i, 0)),
            scratch_shapes=[
                pltpu.VMEM((2, TOP_K, tc, HALF), jnp.uint32),
                pltpu.SemaphoreType.DMA((2,)),
            ],
        ),
        out_shape=jax.ShapeDtypeStruct(h2d.shape, F32),
        compiler_params=_cparams(1),
        name="combine",
    )(pos_flat, h2d, w_top, ys)


def _pick(n, pref):
    t = min(n, pref)
    assert n % t == 0, (n, t)
    return t


def kernel(x, norm1_g, w_in, fox_f_bias, q_norm_g, k_norm_g, ml_i_bias, ml_f_bias, ml_out_norm_g,
           w_branch_a, w_branch_b, w_out, norm2_g, w_router, b_router,
           w_gate, b_gate, w_up, b_up, w_down, b_down):
    bsz, seq, d_model = x.shape
    assert d_model == D_MODEL and norm1_g.shape[0] == 1, "single-layer block of width 2048"
    t_rows = bsz * seq
    x2d = x.reshape(t_rows, D_MODEL).astype(F32)

    offs = [0]
    for wdt in IN_WIDTHS:
        offs.append(offs[-1] + wdt)
    seg = lambda n: w_in[0][:, offs[n]:offs[n + 1]]
    w_main = jnp.concatenate([seg(0), seg(1), seg(2), seg(4), seg(5), seg(6), seg(9), seg(10), seg(11)],
                             axis=1).astype(BF16)
    w_gl = jnp.concatenate([seg(3), seg(7), seg(8),
                            jnp.zeros((D_MODEL, LANES - FOX_HEADS - 2 * ML_HEADS), F32)], axis=1).astype(BF16)
    colscale = jnp.ones((PROJ_W,), F32)
    colscale = colscale.at[COL_FQ:COL_FQ + FOX_W].set(
        jnp.tile(q_norm_g[0].astype(F32), FOX_HEADS) * (FOX_HEAD_DIM ** -0.5 * LOG2_E))
    colscale = colscale.at[COL_FK:COL_FK + FOX_W].set(jnp.tile(k_norm_g[0].astype(F32), FOX_HEADS))
    colscale = colscale.at[COL_MK:COL_MK + ML_QK_W].set(ML_QK_DIM ** -0.5)
    colscale = colscale.reshape(1, PROJ_W)
    gate_bias = jnp.concatenate([fox_f_bias[0], ml_i_bias[0], ml_f_bias[0],
                                 jnp.zeros((LANES - FOX_HEADS - 2 * ML_HEADS,), F32)]).astype(F32).reshape(1, LANES)
    g2 = norm2_g.astype(F32).reshape(1, D_MODEL)

    proj, gates = _in_proj(x2d, norm1_g.astype(F32).reshape(1, D_MODEL), w_main, w_gl, colscale,
                           tm=_pick(t_rows, 1024), tn=512)
    chunk = _pick(seq, 256)
    gp = _gate_prep(gates.reshape(bsz, seq, LANES), gate_bias, ts=_pick(seq, 512), chunk=chunk)
    gp_rows = jnp.transpose(gp[:, :, :16], (0, 2, 1))
    fox_group = 2
    neg_cum_f = (-LOG2_E * gp_rows[:, GCOL_FF:GCOL_FF + FOX_HEADS, :]).reshape(
        bsz * FOX_HEADS // fox_group, fox_group, seq)
    proj3d = proj.reshape(bsz, seq, PROJ_W)
    y_a = _fox_attention(proj3d, neg_cum_f, tq=_pick(seq, 512), heads=fox_group)
    y_b = _mlstm(proj3d, gp, gp_rows, ml_out_norm_g.astype(F32).reshape(1, ML_V_W), chunk=chunk)

    w_r = jnp.concatenate([w_router[0].astype(F32), jnp.zeros((D_MODEL, LANES - N_EXPERTS), F32)], axis=1)
    b_r = jnp.concatenate([b_router[0].astype(F32), jnp.full((LANES - N_EXPERTS,), -jnp.inf, F32)]).reshape(1, LANES)
    h2d, route, counts = _merge(
        y_a.reshape(t_rows, FOX_W), y_b.reshape(t_rows, ML_V_W), proj, x2d,
        w_branch_a[0].astype(BF16), w_branch_b[0].astype(BF16), w_out[0].astype(BF16),
        g2, w_r, b_r, tm=_pick(t_rows, 512))

    tm_e = _pick(t_rows * TOP_K, 1024)
    n_tiles = (t_rows * TOP_K) // tm_e + N_EXPERTS
    top_idx = route[:, 0:TOP_K].astype(jnp.int32)
    top_rank = route[:, TOP_K:2 * TOP_K].astype(jnp.int32)
    top_w = route[:, 2 * TOP_K:3 * TOP_K]
    cnt = counts[0, :N_EXPERTS].astype(jnp.int32)
    tiles_per_e = (cnt + tm_e - 1) // tm_e
    tile_end = jnp.cumsum(tiles_per_e)
    tile_begin = tile_end - tiles_per_e
    pos_flat = ((tile_begin * tm_e)[top_idx] + top_rank).reshape(-1)
    n_used = tile_end[-1:].astype(jnp.int32)
    tile_ids = jnp.arange(n_tiles, dtype=jnp.int32)
    used_ids = jnp.minimum(tile_ids, n_used[0] - 1)
    tile_expert = jnp.sum((used_ids[:, None] >= tile_end[None, :]).astype(jnp.int32), axis=1)
    tile_valid = jnp.clip(cnt[tile_expert] - (used_ids - tile_begin[tile_expert]) * tm_e, 0, tm_e)
    tile_valid = jnp.where(tile_ids < n_used[0], tile_valid, 0).astype(jnp.int32)

    xs = _dispatch(pos_flat, h2d, g2, jnp.zeros((n_tiles * tm_e, HALF), jnp.uint32), tr=_pick(t_rows, 128))
    ys = _experts(tile_expert, tile_valid, n_used, xs,
                  w_gate[0], b_gate[0].reshape(N_EXPERTS, 1, D_FF),
                  w_up[0], b_up[0].reshape(N_EXPERTS, 1, D_FF),
                  w_down[0], b_down[0].reshape(N_EXPERTS, 1, D_MODEL),
                  tm=tm_e, tf=256, sub=_pick(tm_e, 512))
    out = _combine(pos_flat, h2d, top_w, ys, tc=_pick(t_rows, 128))
    return out.reshape(bsz, seq, D_MODEL).astype(x.dtype)
```

```python
import functools

import jax
import jax.numpy as jnp
from jax import lax
from jax.experimental import pallas as pl
from jax.experimental.pallas import tpu as pltpu

F32 = jnp.float32
BF16 = jnp.bfloat16

D_MODEL = 2048
FOX_HEADS = 8
FOX_HEAD_DIM = 128
FOX_W = FOX_HEADS * FOX_HEAD_DIM
ML_HEADS = 4
ML_QK_DIM = 128
ML_V_DIM = 256
ML_QK_W = ML_HEADS * ML_QK_DIM
ML_V_W = ML_HEADS * ML_V_DIM
IGATE_CAP = 15.0
N_EXPERTS = 32
TOP_K = 4
D_FF = 2048
SWIGLU_LIMIT = 7.0
SWIGLU_ALPHA = 1.702
EPS = 1e-5

IN_WIDTHS = (FOX_W, FOX_W, FOX_W, FOX_HEADS,
             ML_QK_W, ML_QK_W, ML_V_W, ML_HEADS, ML_HEADS, ML_V_W,
             D_MODEL, D_MODEL)

LANES = 128
VMEM_LIMIT = 56 * 1024 * 1024

COL_FQ, COL_FK, COL_FV = 0, FOX_W, 2 * FOX_W
COL_MQ = 3 * FOX_W
COL_MK = COL_MQ + ML_QK_W
COL_MV = COL_MK + ML_QK_W
COL_MO = COL_MV + ML_V_W
COL_GA = COL_MO + ML_V_W
COL_GB = COL_GA + D_MODEL
PROJ_W = COL_GB + D_MODEL
GCOL_FF, GCOL_MI, GCOL_MF = 0, FOX_HEADS, FOX_HEADS + ML_HEADS

HALF = D_MODEL // 2
LOG2_E = 1.4426950408889634


def _cparams(n_axes, vmem=VMEM_LIMIT):
    return pltpu.CompilerParams(dimension_semantics=("arbitrary",) * n_axes,
                                vmem_limit_bytes=vmem)


def _log_sigmoid(x):
    return jnp.minimum(x, 0.0) - jnp.log1p(jnp.exp(-jnp.abs(x)))


def _pack_rows(x):
    lo = lax.bitcast_convert_type(x[:, :HALF].astype(BF16).astype(F32), jnp.uint32)
    hi = lax.bitcast_convert_type(x[:, HALF:].astype(BF16).astype(F32), jnp.uint32)
    return hi | (lo >> 16)


def _unpack_rows(p):
    lo = lax.bitcast_convert_type(p << 16, F32)
    hi = lax.bitcast_convert_type(p & jnp.uint32(0xFFFF0000), F32)
    return lo, hi


def _rms_norm_rows(h, g):
    ms = jnp.mean(h * h, axis=-1, keepdims=True)
    return h * lax.rsqrt(ms + EPS) * g


def _in_proj_kernel(x_ref, g1_ref, w_ref, wg_ref, cs_ref, o_ref, gate_ref, xn_ref,
                    *, n_norm_blocks, row_chunk):
    j = pl.program_id(1)
    tm = x_ref.shape[0]
    tn = w_ref.shape[1]

    @pl.when(j == 0)
    def _():
        def body(c, carry):
            r0 = pl.multiple_of(c * row_chunk, row_chunk)
            x = x_ref[pl.ds(r0, row_chunk), :]
            ms = jnp.mean(x * x, axis=-1, keepdims=True)
            xn_ref[pl.ds(r0, row_chunk), :] = (x * lax.rsqrt(ms + EPS) * g1_ref[...]).astype(BF16)
            return carry
        lax.fori_loop(0, tm // row_chunk, body, 0)
        gate_ref[...] = jnp.dot(xn_ref[...], wg_ref[...], preferred_element_type=F32)

    acc = jnp.dot(xn_ref[...], w_ref[...], preferred_element_type=F32)
    cs = cs_ref[...]

    @pl.when(j < n_norm_blocks)
    def _():
        for s in range(tn // LANES):
            a = acc[:, s * LANES:(s + 1) * LANES]
            ms = jnp.mean(a * a, axis=-1, keepdims=True)
            y = a * lax.rsqrt(ms + EPS) * cs[:, s * LANES:(s + 1) * LANES]
            o_ref[:, s * LANES:(s + 1) * LANES] = y.astype(o_ref.dtype)

    @pl.when(j >= n_norm_blocks)
    def _():
        o_ref[...] = (acc * cs).astype(o_ref.dtype)


def _in_proj(x2d, g1, w_main, w_gate, colscale, *, tm, tn):
    t_rows = x2d.shape[0]
    grid = (t_rows // tm, PROJ_W // tn)
    kern = functools.partial(_in_proj_kernel, n_norm_blocks=(2 * FOX_W) // tn, row_chunk=128)
    return pl.pallas_call(
        kern,
        grid=grid,
        in_specs=[
            pl.BlockSpec((tm, D_MODEL), lambda i, j: (i, 0)),
            pl.BlockSpec((1, D_MODEL), lambda i, j: (0, 0)),
            pl.BlockSpec((D_MODEL, tn), lambda i, j: (0, j)),
            pl.BlockSpec((D_MODEL, LANES), lambda i, j: (0, 0)),
            pl.BlockSpec((1, tn), lambda i, j: (0, j)),
        ],
        out_specs=[
            pl.BlockSpec((tm, tn), lambda i, j: (i, j)),
            pl.BlockSpec((tm, LANES), lambda i, j: (i, 0)),
        ],
        out_shape=[
            jax.ShapeDtypeStruct((t_rows, PROJ_W), BF16),
            jax.ShapeDtypeStruct((t_rows, LANES), F32),
        ],
        scratch_shapes=[pltpu.VMEM((tm, D_MODEL), BF16)],
        compiler_params=_cparams(2),
        name="in_proj",
    )(x2d, g1, w_main, w_gate, colscale)


def _gate_prep_kernel(g_ref, b_ref, o_ref, carry_ref, *, chunk):
    s = pl.program_id(1)
    ts = g_ref.shape[1]

    @pl.when(s == 0)
    def _():
        carry_ref[...] = jnp.zeros_like(carry_ref)

    z = g_ref[0] + b_ref[...]
    log_f = _log_sigmoid(z)
    i_pre = IGATE_CAP * jnp.tanh(z / IGATE_CAP)
    row = lax.broadcasted_iota(jnp.int32, (ts, ts), 0)
    col = lax.broadcasted_iota(jnp.int32, (ts, ts), 1)
    tril = (col <= row)
    same_chunk = (row // chunk) == (col // chunk)
    tril_f = jnp.where(tril, 1.0, 0.0).astype(F32)
    tril_c = jnp.where(tril & same_chunk, 1.0, 0.0).astype(F32)
    run_sum = jnp.dot(tril_f, log_f, preferred_element_type=F32,
                      precision=lax.Precision.HIGHEST) + carry_ref[...]
    chunk_sum = jnp.dot(tril_c, log_f, preferred_element_type=F32,
                        precision=lax.Precision.HIGHEST)
    carry_ref[...] = run_sum[ts - 1:ts, :]
    lane = lax.broadcasted_iota(jnp.int32, (ts, LANES), 1)
    o_ref[0] = jnp.where(lane < GCOL_MI, run_sum, jnp.where(lane < GCOL_MF, i_pre, chunk_sum))


def _gate_prep(gates3d, bias, *, ts, chunk):
    bsz, seq, _ = gates3d.shape
    return pl.pallas_call(
        functools.partial(_gate_prep_kernel, chunk=chunk),
        grid=(bsz, seq // ts),
        in_specs=[
            pl.BlockSpec((1, ts, LANES), lambda b, s: (b, s, 0)),
            pl.BlockSpec((1, LANES), lambda b, s: (0, 0)),
        ],
        out_specs=pl.BlockSpec((1, ts, LANES), lambda b, s: (b, s, 0)),
        out_shape=jax.ShapeDtypeStruct((bsz, seq, LANES), F32),
        scratch_shapes=[pltpu.VMEM((1, LANES), F32)],
        compiler_params=_cparams(2),
        name="gate_prep",
    )(gates3d, bias)


def _fox_kernel(lo_ref, q_ref, k_ref, v_ref, ncf_ref, o_ref, *, tq):
    qi = pl.program_id(2)
    first = lo_ref[(pl.program_id(0) * pl.num_programs(1) + pl.program_id(1)) * pl.num_programs(2) + qi]
    hd = FOX_HEAD_DIM
    heads = q_ref.shape[-1] // hd

    def block(kb, carry, masked):
        off = pl.multiple_of(kb * tq, tq)
        new = []
        for hh in range(heads):
            cols = slice(hh * hd, (hh + 1) * hd)
            m, l, acc = carry[hh]
            q = q_ref[0, :, cols]
            k = k_ref[0, pl.ds(off, tq), cols]
            v = v_ref[0, pl.ds(off, tq), cols]
            s = lax.dot_general(q, k, (((1,), (1,)), ((), ())), preferred_element_type=F32)
            s = s + ncf_ref[0, hh:hh + 1, pl.ds(off, tq)]
            if masked:
                row = lax.broadcasted_iota(jnp.int32, (tq, tq), 0)
                col = lax.broadcasted_iota(jnp.int32, (tq, tq), 1)
                s = jnp.where(row >= col, s, -jnp.inf)
            m_new = jnp.maximum(m, jnp.max(s, axis=-1, keepdims=True))
            alpha = jnp.exp2(m - m_new)
            p = jnp.exp2(s - m_new)
            l = alpha * l + jnp.sum(p, axis=-1, keepdims=True)
            acc = alpha * acc + jnp.dot(p.astype(BF16), v, preferred_element_type=F32)
            new.append((m_new, l, acc))
        return tuple(new)

    init = tuple((jnp.full((tq, 1), -jnp.inf, F32), jnp.zeros((tq, 1), F32), jnp.zeros((tq, hd), F32))
                 for _ in range(heads))
    carry = lax.fori_loop(first, qi, lambda kb, c: block(kb, c, False), init)
    final = block(qi, carry, True)
    for hh in range(heads):
        _, l, acc = final[hh]
        o_ref[0, :, hh * hd:(hh + 1) * hd] = (acc / l).astype(o_ref.dtype)


FOX_PRUNE_MARGIN = 150.0


def _fox_first_block(neg_cum_f, qk_bound, *, tq, heads):
    n_seq, _, seq = neg_cum_f.shape[0], neg_cum_f.shape[1], neg_cum_f.shape[2]
    nq = seq // tq
    blk = neg_cum_f.reshape(n_seq, heads, nq, tq)
    blk_max = lax.cummax(jnp.max(blk, axis=-1), axis=2)
    row_min = jnp.min(blk, axis=-1)
    thresh = row_min - (2.0 * qk_bound + FOX_PRUNE_MARGIN)
    skippable = blk_max[:, :, None, :] < thresh[:, :, :, None]
    first = jnp.sum(skippable.astype(jnp.int32), axis=-1)
    first = jnp.min(first, axis=1)
    first = jnp.minimum(first, jnp.arange(nq, dtype=jnp.int32)[None, :])
    return first.reshape(-1).astype(jnp.int32)


def _fox_attention(proj3d, neg_cum_f, qk_bound, *, tq, heads):
    bsz, seq, _ = proj3d.shape
    hw = heads * FOX_HEAD_DIM
    groups = FOX_HEADS // heads
    first_block = _fox_first_block(neg_cum_f, qk_bound, tq=tq, heads=heads)
    return pl.pallas_call(
        functools.partial(_fox_kernel, tq=tq),
        grid_spec=pltpu.PrefetchScalarGridSpec(
            num_scalar_prefetch=1,
            grid=(bsz, groups, seq // tq),
            in_specs=[
                pl.BlockSpec((1, tq, hw), lambda b, g, i, lo: (b, i, COL_FQ // hw + g)),
                pl.BlockSpec((1, seq, hw), lambda b, g, i, lo: (b, 0, COL_FK // hw + g)),
                pl.BlockSpec((1, seq, hw), lambda b, g, i, lo: (b, 0, COL_FV // hw + g)),
                pl.BlockSpec((1, heads, seq), lambda b, g, i, lo: (b * groups + g, 0, 0)),
            ],
            out_specs=pl.BlockSpec((1, tq, hw), lambda b, g, i, lo: (b, i, g)),
        ),
        out_shape=jax.ShapeDtypeStruct((bsz, seq, FOX_W), BF16),
        compiler_params=_cparams(3),
        name="fox_attn",
    )(first_block, proj3d, proj3d, proj3d, neg_cum_f)


def _mlstm_kernel(q_ref, k_ref, v_ref, mo_ref, gc_ref, gr_ref, gout_ref, o_ref,
                  c_ref, n_ref, m_ref):
    c_idx = pl.program_id(1)
    L = q_ref.shape[1]

    @pl.when(c_idx == 0)
    def _():
        c_ref[...] = jnp.zeros_like(c_ref)
        n_ref[...] = jnp.zeros_like(n_ref)
        m_ref[...] = jnp.zeros_like(m_ref)

    row = lax.broadcasted_iota(jnp.int32, (L, L), 0)
    col = lax.broadcasted_iota(jnp.int32, (L, L), 1)
    tril = col <= row

    for h in range(ML_HEADS):
        qs = slice(h * ML_QK_DIM, (h + 1) * ML_QK_DIM)
        vs = slice(h * ML_V_DIM, (h + 1) * ML_V_DIM)
        q = q_ref[0, :, qs]
        k = k_ref[0, :, qs]
        v = v_ref[0, :, vs]
        b_col = gc_ref[0, :, GCOL_MF + h:GCOL_MF + h + 1]
        i_col = gc_ref[0, :, GCOL_MI + h:GCOL_MI + h + 1]
        b_row = gr_ref[0, GCOL_MF + h:GCOL_MF + h + 1, :]
        i_row = gr_ref[0, GCOL_MI + h:GCOL_MI + h + 1, :]
        m_prev = m_ref[h, 0:1, 0:1]
        c_prev = c_ref[h]
        n_prev = n_ref[h]

        d_log = jnp.where(tril, b_col - b_row + i_row, -jnp.inf)
        g_inter = b_col + m_prev
        m_row = jnp.maximum(g_inter, jnp.max(d_log, axis=-1, keepdims=True))
        w_intra = jnp.exp(d_log - m_row)
        w_inter = jnp.exp(g_inter - m_row)
        qk = lax.dot_general(q, k, (((1,), (1,)), ((), ())), preferred_element_type=F32)
        scores = qk * w_intra
        num = (w_inter * jnp.dot(q, c_prev.astype(BF16), preferred_element_type=F32)
               + jnp.dot(scores.astype(BF16), v, preferred_element_type=F32))
        den = (w_inter * jnp.sum(q.astype(F32) * n_prev, axis=-1, keepdims=True)
               + jnp.sum(scores, axis=-1, keepdims=True))
        hh = num / jnp.maximum(jnp.abs(den), jnp.exp(-m_row))

        b_last = b_col[L - 1:L, :]
        a_log = b_last - b_col + i_col
        m_new = jnp.maximum(b_last + m_prev, jnp.max(a_log, axis=0, keepdims=True))
        decay = jnp.exp(b_last + m_prev - m_new)
        w_upd = jnp.exp(a_log - m_new)
        kw = k.astype(F32) * w_upd
        c_ref[h] = decay * c_prev + jnp.dot(kw.T.astype(BF16), v, preferred_element_type=F32)
        n_ref[h] = decay * n_prev + jnp.sum(kw, axis=0, keepdims=True)
        m_ref[h] = jnp.broadcast_to(m_new, m_ref.shape[1:])

        ms = jnp.mean(hh * hh, axis=-1, keepdims=True)
        y = (hh * lax.rsqrt(ms + EPS) * gout_ref[:, vs]
             * jax.nn.sigmoid(mo_ref[0, :, vs].astype(F32)))
        o_ref[0, :, vs] = y.astype(o_ref.dtype)


def _mlstm(proj3d, gate_cols, gate_rows, gout, *, chunk):
    bsz, seq, _ = proj3d.shape
    return pl.pallas_call(
        _mlstm_kernel,
        grid=(bsz, seq // chunk),
        in_specs=[
            pl.BlockSpec((1, chunk, ML_QK_W), lambda b, c: (b, c, COL_MQ // ML_QK_W)),
            pl.BlockSpec((1, chunk, ML_QK_W), lambda b, c: (b, c, COL_MK // ML_QK_W)),
            pl.BlockSpec((1, chunk, ML_V_W), lambda b, c: (b, c, COL_MV // ML_V_W)),
            pl.BlockSpec((1, chunk, ML_V_W), lambda b, c: (b, c, COL_MO // ML_V_W)),
            pl.BlockSpec((1, chunk, LANES), lambda b, c: (b, c, 0)),
            pl.BlockSpec((1, 16, chunk), lambda b, c: (b, 0, c)),
            pl.BlockSpec((1, ML_V_W), lambda b, c: (0, 0)),
        ],
        out_specs=pl.BlockSpec((1, chunk, ML_V_W), lambda b, c: (b, c, 0)),
        out_shape=jax.ShapeDtypeStruct((bsz, seq, ML_V_W), BF16),
        scratch_shapes=[
            pltpu.VMEM((ML_HEADS, ML_QK_DIM, ML_V_DIM), F32),
            pltpu.VMEM((ML_HEADS, 1, ML_QK_DIM), F32),
            pltpu.VMEM((ML_HEADS, 8, LANES), F32),
        ],
        compiler_params=_cparams(2),
        name="mlstm",
    )(proj3d, proj3d, proj3d, proj3d, gate_cols, gate_rows, gout)


def _merge_kernel(ya_ref, yb_ref, ga_ref, gb_ref, x_ref, wa_ref, wb_ref, wo_ref, g2_ref,
                  wr_ref, br_ref, h_ref, tp_ref, route_ref, cnt_ref, carry_ref):
    i = pl.program_id(0)
    tm = x_ref.shape[0]

    @pl.when(i == 0)
    def _():
        carry_ref[...] = jnp.zeros_like(carry_ref)

    a = jnp.dot(ya_ref[...], wa_ref[...], preferred_element_type=F32)
    b = jnp.dot(yb_ref[...], wb_ref[...], preferred_element_type=F32)
    merged = (jax.nn.sigmoid(ga_ref[...].astype(F32)) * a
              + jax.nn.sigmoid(gb_ref[...].astype(F32)) * b)
    h = x_ref[...] + jnp.dot(merged.astype(BF16), wo_ref[...], preferred_element_type=F32)
    h_ref[...] = h
    t = _rms_norm_rows(h, g2_ref[...])
    tp_ref[...] = _pack_rows(t)

    t_hi = t.astype(BF16)
    t_lo = (t - t_hi.astype(F32)).astype(BF16)
    w_r = wr_ref[...]
    w_hi = w_r.astype(BF16)
    w_lo = (w_r - w_hi.astype(F32)).astype(BF16)
    logits = (jnp.dot(t_hi, w_hi, preferred_element_type=F32)
              + jnp.dot(t_lo, w_hi, preferred_element_type=F32)
              + jnp.dot(t_hi, w_lo, preferred_element_type=F32)
              + br_ref[...])
    lane = lax.broadcasted_iota(jnp.int32, (tm, LANES), 1).astype(F32)
    lg = logits
    sels, vals, idxs = [], [], []
    for _ in range(TOP_K):
        mx = jnp.max(lg, axis=-1, keepdims=True)
        ik = jnp.min(jnp.where(lg == mx, lane, float(LANES)), axis=-1, keepdims=True)
        sel = lane == ik
        sels.append(sel)
        vals.append(mx)
        idxs.append(ik)
        lg = jnp.where(sel, -jnp.inf, lg)
    exps = [jnp.exp(v - vals[0]) for v in vals]
    den = exps[0] + exps[1] + exps[2] + exps[3]
    mask = jnp.zeros((tm, LANES), F32)
    for sel in sels:
        mask = mask + jnp.where(sel, 1.0, 0.0)
    row = lax.broadcasted_iota(jnp.int32, (tm, tm), 0)
    col = lax.broadcasted_iota(jnp.int32, (tm, tm), 1)
    strict = jnp.where(col < row, 1.0, 0.0).astype(BF16)
    ranks = jnp.dot(strict, mask.astype(BF16), preferred_element_type=F32) + carry_ref[...]
    slab = jnp.zeros((tm, LANES), F32)
    for kk in range(TOP_K):
        rank_k = jnp.sum(jnp.where(sels[kk], ranks, 0.0), axis=-1, keepdims=True)
        slab = jnp.where(lane == float(kk), idxs[kk], slab)
        slab = jnp.where(lane == float(TOP_K + kk), rank_k, slab)
        slab = jnp.where(lane == float(2 * TOP_K + kk), exps[kk] / den, slab)
    route_ref[...] = slab
    new_carry = carry_ref[...] + jnp.sum(mask, axis=0, keepdims=True)
    carry_ref[...] = new_carry
    cnt_ref[...] = new_carry


def _merge(y_a, y_b, proj, x2d, w_a, w_b, w_o, g2, w_r, b_r, *, tm):
    t_rows = x2d.shape[0]
    const = lambda shape: pl.BlockSpec(shape, lambda i: (0, 0), pipeline_mode=pl.Buffered(1))
    return pl.pallas_call(
        _merge_kernel,
        grid=(t_rows // tm,),
        in_specs=[
            pl.BlockSpec((tm, FOX_W), lambda i: (i, 0)),
            pl.BlockSpec((tm, ML_V_W), lambda i: (i, 0)),
            pl.BlockSpec((tm, D_MODEL), lambda i: (i, COL_GA // D_MODEL)),
            pl.BlockSpec((tm, D_MODEL), lambda i: (i, COL_GB // D_MODEL)),
            pl.BlockSpec((tm, D_MODEL), lambda i: (i, 0)),
            const((FOX_W, D_MODEL)),
            const((ML_V_W, D_MODEL)),
            const((D_MODEL, D_MODEL)),
            const((1, D_MODEL)),
            const((D_MODEL, LANES)),
            const((1, LANES)),
        ],
        out_specs=[
            pl.BlockSpec((tm, D_MODEL), lambda i: (i, 0)),
            pl.BlockSpec((tm, HALF), lambda i: (i, 0)),
            pl.BlockSpec((tm, LANES), lambda i: (i, 0)),
            pl.BlockSpec((1, LANES), lambda i: (0, 0)),
        ],
        out_shape=[
            jax.ShapeDtypeStruct((t_rows, D_MODEL), F32),
            jax.ShapeDtypeStruct((t_rows, HALF), jnp.uint32),
            jax.ShapeDtypeStruct((t_rows, LANES), F32),
            jax.ShapeDtypeStruct((1, LANES), F32),
        ],
        scratch_shapes=[pltpu.VMEM((1, LANES), F32)],
        compiler_params=_cparams(1),
        name="merge_router",
    )(y_a, y_b, proj, proj, x2d, w_a, w_b, w_o, g2, w_r, b_r)


def _dispatch_kernel(pos_ref, zt_ref, tp_ref, xs_ref, zero_ref, sem, zsem, *, tr, tile):
    i = pl.program_id(0)
    n = pl.num_programs(0)

    @pl.when(i == 0)
    def _():
        zero_ref[...] = jnp.zeros_like(zero_ref)

        def zero_copy(z):
            start = pl.multiple_of(jnp.maximum(zt_ref[z], 0) * tile, tile)
            return pltpu.make_async_copy(zero_ref, xs_ref.at[pl.ds(start, tile), :], zsem)

        def start_body(z, carry):
            @pl.when(zt_ref[z] >= 0)
            def _():
                zero_copy(z).start()
            return carry
        lax.fori_loop(0, zt_ref.shape[0], start_body, 0)

        def wait_body(z, carry):
            @pl.when(zt_ref[z] >= 0)
            def _():
                zero_copy(z).wait()
            return carry
        lax.fori_loop(0, zt_ref.shape[0], wait_body, 0)

    def row_copy(s, tok, p):
        return pltpu.make_async_copy(tp_ref.at[pl.ds(tok, 1), :], xs_ref.at[pl.ds(p, 1), :], sem.at[s])

    slot = i % 2
    tok0 = i * tr

    def issue(r, carry):
        for kk in range(TOP_K):
            row_copy(slot, tok0 + r, pos_ref[(tok0 + r) * TOP_K + kk]).start()
        return carry
    lax.fori_loop(0, tr, issue, 0)

    def drain(s):
        def body(r, carry):
            for kk in range(TOP_K):
                row_copy(s, 0, 0).wait()
            return carry
        lax.fori_loop(0, tr, body, 0)

    @pl.when(i > 0)
    def _():
        drain(1 - slot)

    @pl.when(i == n - 1)
    def _():
        drain(slot)


def _dispatch(pos_flat, zero_tiles, tpk, *, n_rows, tr, tile):
    t_rows = tpk.shape[0]
    return pl.pallas_call(
        functools.partial(_dispatch_kernel, tr=tr, tile=tile),
        grid_spec=pltpu.PrefetchScalarGridSpec(
            num_scalar_prefetch=2,
            grid=(t_rows // tr,),
            in_specs=[pl.BlockSpec(memory_space=pl.ANY)],
            out_specs=pl.BlockSpec(memory_space=pl.ANY),
            scratch_shapes=[
                pltpu.VMEM((tile, HALF), jnp.uint32),
                pltpu.SemaphoreType.DMA((2,)),
                pltpu.SemaphoreType.DMA(()),
            ],
        ),
        out_shape=jax.ShapeDtypeStruct((n_rows, HALF), jnp.uint32),
        compiler_params=_cparams(1),
        name="dispatch",
    )(pos_flat, zero_tiles, tpk)


def _experts_kernel(te_ref, nv_ref, nu_ref, xs_ref, wg_ref, bg_ref, wu_ref, bu_ref, wd_ref, bd_ref,
                    o_ref, xb_ref, acc_ref, *, sub):
    del te_ref, nu_ref
    i = pl.program_id(0)
    j = pl.program_id(1)
    n_ff = pl.num_programs(1)
    tm = xs_ref.shape[0]
    valid = nv_ref[i]

    @pl.when(j == 0)
    def _():
        lo, hi = _unpack_rows(xs_ref[...])
        xb_ref[:, :HALF] = lo.astype(BF16)
        xb_ref[:, HALF:] = hi.astype(BF16)
        acc_ref[...] = jnp.broadcast_to(bd_ref[0], acc_ref.shape)

    def ffn_chunk(n_rows):
        xb = xb_ref[0:n_rows, :]
        g = jnp.dot(xb, wg_ref[0].astype(BF16), preferred_element_type=F32) + bg_ref[0]
        u = jnp.dot(xb, wu_ref[0].astype(BF16), preferred_element_type=F32) + bu_ref[0]
        g = jnp.minimum(g, SWIGLU_LIMIT)
        u = jnp.clip(u, -SWIGLU_LIMIT, SWIGLU_LIMIT)
        act = (u + 1.0) * g * jax.nn.sigmoid(SWIGLU_ALPHA * g)
        acc_ref[0:n_rows, :] += jnp.dot(act.astype(BF16), wd_ref[0].astype(BF16),
                                        preferred_element_type=F32)

    n_sub = tm // sub
    for s in range(n_sub):
        lo_rows, hi_rows = s * sub, (s + 1) * sub
        upper = (valid <= hi_rows) if s + 1 < n_sub else True

        @pl.when((valid > lo_rows) & upper)
        def _():
            ffn_chunk(hi_rows)

    @pl.when(j == n_ff - 1)
    def _():
        o_ref[...] = _pack_rows(acc_ref[...])


def _experts(tile_expert, tile_valid, n_used, xs, w_gate, b_gate, w_up, b_up, w_down, b_down,
             *, tm, tf, sub):
    n_tiles = xs.shape[0] // tm
    n_ff = D_FF // tf

    def tile_i(i, nu):
        return jnp.maximum(jnp.minimum(i, nu[0] - 1), 0)

    def ff_j(i, j, nu):
        return jnp.where(i < nu[0], j, n_ff - 1)

    return pl.pallas_call(
        functools.partial(_experts_kernel, sub=sub),
        grid_spec=pltpu.PrefetchScalarGridSpec(
            num_scalar_prefetch=3,
            grid=(n_tiles, n_ff),
            in_specs=[
                pl.BlockSpec((tm, HALF), lambda i, j, te, nv, nu: (tile_i(i, nu), 0)),
                pl.BlockSpec((1, D_MODEL, tf), lambda i, j, te, nv, nu: (te[i], 0, ff_j(i, j, nu))),
                pl.BlockSpec((1, 1, tf), lambda i, j, te, nv, nu: (te[i], 0, ff_j(i, j, nu))),
                pl.BlockSpec((1, D_MODEL, tf), lambda i, j, te, nv, nu: (te[i], 0, ff_j(i, j, nu))),
                pl.BlockSpec((1, 1, tf), lambda i, j, te, nv, nu: (te[i], 0, ff_j(i, j, nu))),
                pl.BlockSpec((1, tf, D_MODEL), lambda i, j, te, nv, nu: (te[i], ff_j(i, j, nu), 0)),
                pl.BlockSpec((1, 1, D_MODEL), lambda i, j, te, nv, nu: (te[i], 0, 0)),
            ],
            out_specs=pl.BlockSpec((tm, HALF), lambda i, j, te, nv, nu: (i, 0)),
            scratch_shapes=[
                pltpu.VMEM((tm, D_MODEL), BF16),
                pltpu.VMEM((tm, D_MODEL), F32),
            ],
        ),
        out_shape=jax.ShapeDtypeStruct(xs.shape, jnp.uint32),
        compiler_params=_cparams(2),
        name="experts",
    )(tile_expert, tile_valid, n_used, xs, w_gate, b_gate, w_up, b_up, w_down, b_down)


def _combine_kernel(pos_ref, h_ref, w_ref, ys_ref, o_ref, buf_ref, sem):
    i = pl.program_id(0)
    n = pl.num_programs(0)
    tc = h_ref.shape[0]

    def row_copy(p, slot, kk, r):
        return pltpu.make_async_copy(ys_ref.at[pl.ds(p, 1), :],
                                     buf_ref.at[slot, kk, pl.ds(r, 1), :], sem.at[slot])

    def issue(step, slot):
        base = step * (tc * TOP_K)

        def body(r, carry):
            for kk in range(TOP_K):
                row_copy(pos_ref[base + r * TOP_K + kk], slot, kk, r).start()
            return carry
        lax.fori_loop(0, tc, body, 0)

    @pl.when(i == 0)
    def _():
        issue(0, 0)

    @pl.when(i + 1 < n)
    def _():
        issue(i + 1, (i + 1) % 2)

    slot = i % 2

    def drain(r, carry):
        for kk in range(TOP_K):
            row_copy(0, slot, kk, r).wait()
        return carry
    lax.fori_loop(0, tc, drain, 0)

    w = w_ref[...]
    acc_lo = h_ref[:, :HALF]
    acc_hi = h_ref[:, HALF:]
    for kk in range(TOP_K):
        lo, hi = _unpack_rows(buf_ref[slot, kk])
        acc_lo = acc_lo + w[:, kk:kk + 1] * lo
        acc_hi = acc_hi + w[:, kk:kk + 1] * hi
    o_ref[:, :HALF] = acc_lo
    o_ref[:, HALF:] = acc_hi


def _combine(pos_flat, h2d, w_top, ys, *, tc):
    t_rows = h2d.shape[0]
    return pl.pallas_call(
        _combine_kernel,
        grid_spec=pltpu.PrefetchScalarGridSpec(
            num_scalar_prefetch=1,
            grid=(t_rows // tc,),
            in_specs=[
                pl.BlockSpec((tc, D_MODEL), lambda i, pos: (i, 0)),
                pl.BlockSpec((tc, TOP_K), lambda i, pos: (i, 0)),
                pl.BlockSpec(memory_space=pl.ANY),
            ],
            out_specs=pl.BlockSpec((tc, D_MODEL), lambda i, pos: (i, 0)),
            scratch_shapes=[
                pltpu.VMEM((2, TOP_K, tc, HALF), jnp.uint32),
                pltpu.SemaphoreType.DMA((2,)),
            ],
        ),
        out_shape=jax.ShapeDtypeStruct(h2d.shape, F32),
        compiler_params=_cparams(1),
        name="combine",
    )(pos_flat, h2d, w_top, ys)


def _pick(n, pref):
    t = min(n, pref)
    assert n % t == 0, (n, t)
    return t


def kernel(x, norm1_g, w_in, fox_f_bias, q_norm_g, k_norm_g, ml_i_bias, ml_f_bias, ml_out_norm_g,
           w_branch_a, w_branch_b, w_out, norm2_g, w_router, b_router,
           w_gate, b_gate, w_up, b_up, w_down, b_down):
    bsz, seq, d_model = x.shape
    assert d_model == D_MODEL and norm1_g.shape[0] == 1, "single-layer block of width 2048"
    t_rows = bsz * seq
    x2d = x.reshape(t_rows, D_MODEL).astype(F32)

    offs = [0]
    for wdt in IN_WIDTHS:
        offs.append(offs[-1] + wdt)
    seg = lambda n: w_in[0][:, offs[n]:offs[n + 1]]
    w_main = jnp.concatenate([seg(0), seg(1), seg(2), seg(4), seg(5), seg(6), seg(9), seg(10), seg(11)],
                             axis=1).astype(BF16)
    w_gl = jnp.concatenate([seg(3), seg(7), seg(8),
                            jnp.zeros((D_MODEL, LANES - FOX_HEADS - 2 * ML_HEADS), F32)], axis=1).astype(BF16)
    colscale = jnp.ones((PROJ_W,), F32)
    colscale = colscale.at[COL_FQ:COL_FQ + FOX_W].set(
        jnp.tile(q_norm_g[0].astype(F32), FOX_HEADS) * (FOX_HEAD_DIM ** -0.5 * LOG2_E))
    colscale = colscale.at[COL_FK:COL_FK + FOX_W].set(jnp.tile(k_norm_g[0].astype(F32), FOX_HEADS))
    colscale = colscale.at[COL_MK:COL_MK + ML_QK_W].set(ML_QK_DIM ** -0.5)
    colscale = colscale.reshape(1, PROJ_W)
    gate_bias = jnp.concatenate([fox_f_bias[0], ml_i_bias[0], ml_f_bias[0],
                                 jnp.zeros((LANES - FOX_HEADS - 2 * ML_HEADS,), F32)]).astype(F32).reshape(1, LANES)
    g2 = norm2_g.astype(F32).reshape(1, D_MODEL)

    proj, gates = _in_proj(x2d, norm1_g.astype(F32).reshape(1, D_MODEL), w_main, w_gl, colscale,
                           tm=_pick(t_rows, 1024), tn=512)
    chunk = _pick(seq, 256)
    gp = _gate_prep(gates.reshape(bsz, seq, LANES), gate_bias, ts=_pick(seq, 512), chunk=chunk)
    gp_rows = jnp.transpose(gp[:, :, :16], (0, 2, 1))
    fox_group = 2
    neg_cum_f = (-LOG2_E * gp_rows[:, GCOL_FF:GCOL_FF + FOX_HEADS, :]).reshape(
        bsz * FOX_HEADS // fox_group, fox_group, seq)
    proj3d = proj.reshape(bsz, seq, PROJ_W)
    qk_bound = (1.02 * FOX_HEAD_DIM * (FOX_HEAD_DIM ** -0.5 * LOG2_E)
                * jnp.max(jnp.abs(q_norm_g[0].astype(F32))) * jnp.max(jnp.abs(k_norm_g[0].astype(F32))))
    y_a = _fox_attention(proj3d, neg_cum_f, qk_bound, tq=_pick(seq, 512), heads=fox_group)
    y_b = _mlstm(proj3d, gp, gp_rows, ml_out_norm_g.astype(F32).reshape(1, ML_V_W), chunk=chunk)

    w_r = jnp.concatenate([w_router[0].astype(F32), jnp.zeros((D_MODEL, LANES - N_EXPERTS), F32)], axis=1)
    b_r = jnp.concatenate([b_router[0].astype(F32), jnp.full((LANES - N_EXPERTS,), -jnp.inf, F32)]).reshape(1, LANES)
    h2d, tpk, route, counts = _merge(
        y_a.reshape(t_rows, FOX_W), y_b.reshape(t_rows, ML_V_W), proj, x2d,
        w_branch_a[0].astype(BF16), w_branch_b[0].astype(BF16), w_out[0].astype(BF16),
        g2, w_r, b_r, tm=_pick(t_rows, 512))

    tm_e = _pick(t_rows * TOP_K, 1024)
    n_tiles = (t_rows * TOP_K) // tm_e + N_EXPERTS
    top_idx = route[:, 0:TOP_K].astype(jnp.int32)
    top_rank = route[:, TOP_K:2 * TOP_K].astype(jnp.int32)
    top_w = route[:, 2 * TOP_K:3 * TOP_K]
    cnt = counts[0, :N_EXPERTS].astype(jnp.int32)
    tiles_per_e = (cnt + tm_e - 1) // tm_e
    tile_end = jnp.cumsum(tiles_per_e)
    tile_begin = tile_end - tiles_per_e
    pos_flat = ((tile_begin * tm_e)[top_idx] + top_rank).reshape(-1)
    n_used = tile_end[-1:].astype(jnp.int32)
    tile_ids = jnp.arange(n_tiles, dtype=jnp.int32)
    used_ids = jnp.minimum(tile_ids, n_used[0] - 1)
    tile_expert = jnp.sum((used_ids[:, None] >= tile_end[None, :]).astype(jnp.int32), axis=1)
    tile_valid = jnp.clip(cnt[tile_expert] - (used_ids - tile_begin[tile_expert]) * tm_e, 0, tm_e)
    tile_valid = jnp.where(tile_ids < n_used[0], tile_valid, 0).astype(jnp.int32)

    last_tile = jnp.where(tiles_per_e > 0, tile_end - 1, -1)
    tail_ids = tile_ids[n_tiles - N_EXPERTS:]
    zero_tiles = jnp.concatenate([last_tile, jnp.where(tail_ids >= n_used[0], tail_ids, -1)]).astype(jnp.int32)
    xs = _dispatch(pos_flat, zero_tiles, tpk, n_rows=n_tiles * tm_e, tr=_pick(t_rows, 128), tile=tm_e)
    ys = _experts(tile_expert, tile_valid, n_used, xs,
                  w_gate[0], b_gate[0].reshape(N_EXPERTS, 1, D_FF),
                  w_up[0], b_up[0].reshape(N_EXPERTS, 1, D_FF),
                  w_down[0], b_down[0].reshape(N_EXPERTS, 1, D_MODEL),
                  tm=tm_e, tf=256, sub=_pick(tm_e, 512))
    out = _combine(pos_flat, h2d, top_w, ys, tc=_pick(t_rows, 128))
    return out.reshape(bsz, seq, D_MODEL).astype(x.dtype)
```

```python
import functools

import jax
import jax.numpy as jnp
from jax import lax
from jax.experimental import pallas as pl
from jax.experimental.pallas import tpu as pltpu

F32 = jnp.float32
BF16 = jnp.bfloat16

D_MODEL = 2048
FOX_HEADS = 8
FOX_HEAD_DIM = 128
FOX_W = FOX_HEADS * FOX_HEAD_DIM
ML_HEADS = 4
ML_QK_DIM = 128
ML_V_DIM = 256
ML_QK_W = ML_HEADS * ML_QK_DIM
ML_V_W = ML_HEADS * ML_V_DIM
IGATE_CAP = 15.0
N_EXPERTS = 32
TOP_K = 4
D_FF = 2048
SWIGLU_LIMIT = 7.0
SWIGLU_ALPHA = 1.702
EPS = 1e-5

IN_WIDTHS = (FOX_W, FOX_W, FOX_W, FOX_HEADS,
             ML_QK_W, ML_QK_W, ML_V_W, ML_HEADS, ML_HEADS, ML_V_W,
             D_MODEL, D_MODEL)

LANES = 128
VMEM_LIMIT = 56 * 1024 * 1024

COL_FQ, COL_FK, COL_FV = 0, FOX_W, 2 * FOX_W
COL_MQ = 3 * FOX_W
COL_MK = COL_MQ + ML_QK_W
COL_MV = COL_MK + ML_QK_W
COL_MO = COL_MV + ML_V_W
COL_GA = COL_MO + ML_V_W
COL_GB = COL_GA + D_MODEL
PROJ_W = COL_GB + D_MODEL
GCOL_FF, GCOL_MI, GCOL_MF = 0, FOX_HEADS, FOX_HEADS + ML_HEADS

HALF = D_MODEL // 2
LOG2_E = 1.4426950408889634


def _cparams(n_axes, vmem=VMEM_LIMIT):
    return pltpu.CompilerParams(dimension_semantics=("arbitrary",) * n_axes,
                                vmem_limit_bytes=vmem)


def _log_sigmoid(x):
    return jnp.minimum(x, 0.0) - jnp.log1p(jnp.exp(-jnp.abs(x)))


def _pack_rows(x):
    lo = lax.bitcast_convert_type(x[:, :HALF].astype(BF16).astype(F32), jnp.uint32)
    hi = lax.bitcast_convert_type(x[:, HALF:].astype(BF16).astype(F32), jnp.uint32)
    return hi | (lo >> 16)


def _unpack_rows(p):
    lo = lax.bitcast_convert_type(p << 16, F32)
    hi = lax.bitcast_convert_type(p & jnp.uint32(0xFFFF0000), F32)
    return lo, hi


def _rms_norm_rows(h, g):
    ms = jnp.mean(h * h, axis=-1, keepdims=True)
    return h * lax.rsqrt(ms + EPS) * g


def _in_proj_kernel(x_ref, g1_ref, w_ref, wg_ref, cs_ref, o_ref, gate_ref, xn_ref,
                    *, n_norm_blocks, row_chunk):
    j = pl.program_id(1)
    tm = x_ref.shape[0]
    tn = w_ref.shape[1]

    @pl.when(j == 0)
    def _():
        def body(c, carry):
            r0 = pl.multiple_of(c * row_chunk, row_chunk)
            x = x_ref[pl.ds(r0, row_chunk), :]
            ms = jnp.mean(x * x, axis=-1, keepdims=True)
            xn_ref[pl.ds(r0, row_chunk), :] = (x * lax.rsqrt(ms + EPS) * g1_ref[...]).astype(BF16)
            return carry
        lax.fori_loop(0, tm // row_chunk, body, 0)
        gate_ref[...] = jnp.dot(xn_ref[...], wg_ref[...], preferred_element_type=F32)

    acc = jnp.dot(xn_ref[...], w_ref[...], preferred_element_type=F32)
    cs = cs_ref[...]

    @pl.when(j < n_norm_blocks)
    def _():
        for s in range(tn // LANES):
            a = acc[:, s * LANES:(s + 1) * LANES]
            ms = jnp.mean(a * a, axis=-1, keepdims=True)
            y = a * lax.rsqrt(ms + EPS) * cs[:, s * LANES:(s + 1) * LANES]
            o_ref[:, s * LANES:(s + 1) * LANES] = y.astype(o_ref.dtype)

    @pl.when(j >= n_norm_blocks)
    def _():
        o_ref[...] = (acc * cs).astype(o_ref.dtype)


def _in_proj(x2d, g1, w_main, w_gate, colscale, *, tm, tn):
    t_rows = x2d.shape[0]
    grid = (t_rows // tm, PROJ_W // tn)
    kern = functools.partial(_in_proj_kernel, n_norm_blocks=(2 * FOX_W) // tn, row_chunk=128)
    return pl.pallas_call(
        kern,
        grid=grid,
        in_specs=[
            pl.BlockSpec((tm, D_MODEL), lambda i, j: (i, 0)),
            pl.BlockSpec((1, D_MODEL), lambda i, j: (0, 0)),
            pl.BlockSpec((D_MODEL, tn), lambda i, j: (0, j)),
            pl.BlockSpec((D_MODEL, LANES), lambda i, j: (0, 0)),
            pl.BlockSpec((1, tn), lambda i, j: (0, j)),
        ],
        out_specs=[
            pl.BlockSpec((tm, tn), lambda i, j: (i, j)),
            pl.BlockSpec((tm, LANES), lambda i, j: (i, 0)),
        ],
        out_shape=[
            jax.ShapeDtypeStruct((t_rows, PROJ_W), BF16),
            jax.ShapeDtypeStruct((t_rows, LANES), F32),
        ],
        scratch_shapes=[pltpu.VMEM((tm, D_MODEL), BF16)],
        compiler_params=_cparams(2),
        name="in_proj",
    )(x2d, g1, w_main, w_gate, colscale)


def _gate_prep_kernel(g_ref, b_ref, o_ref, carry_ref, *, chunk):
    s = pl.program_id(1)
    ts = g_ref.shape[1]

    @pl.when(s == 0)
    def _():
        carry_ref[...] = jnp.zeros_like(carry_ref)

    z = g_ref[0] + b_ref[...]
    log_f = _log_sigmoid(z)
    i_pre = IGATE_CAP * jnp.tanh(z / IGATE_CAP)
    row = lax.broadcasted_iota(jnp.int32, (ts, ts), 0)
    col = lax.broadcasted_iota(jnp.int32, (ts, ts), 1)
    tril = (col <= row)
    same_chunk = (row // chunk) == (col // chunk)
    tril_f = jnp.where(tril, 1.0, 0.0).astype(F32)
    tril_c = jnp.where(tril & same_chunk, 1.0, 0.0).astype(F32)
    run_sum = jnp.dot(tril_f, log_f, preferred_element_type=F32,
                      precision=lax.Precision.HIGHEST) + carry_ref[...]
    chunk_sum = jnp.dot(tril_c, log_f, preferred_element_type=F32,
                        precision=lax.Precision.HIGHEST)
    carry_ref[...] = run_sum[ts - 1:ts, :]
    lane = lax.broadcasted_iota(jnp.int32, (ts, LANES), 1)
    o_ref[0] = jnp.where(lane < GCOL_MI, run_sum, jnp.where(lane < GCOL_MF, i_pre, chunk_sum))


def _gate_prep(gates3d, bias, *, ts, chunk):
    bsz, seq, _ = gates3d.shape
    return pl.pallas_call(
        functools.partial(_gate_prep_kernel, chunk=chunk),
        grid=(bsz, seq // ts),
        in_specs=[
            pl.BlockSpec((1, ts, LANES), lambda b, s: (b, s, 0)),
            pl.BlockSpec((1, LANES), lambda b, s: (0, 0)),
        ],
        out_specs=pl.BlockSpec((1, ts, LANES), lambda b, s: (b, s, 0)),
        out_shape=jax.ShapeDtypeStruct((bsz, seq, LANES), F32),
        scratch_shapes=[pltpu.VMEM((1, LANES), F32)],
        compiler_params=_cparams(2),
        name="gate_prep",
    )(gates3d, bias)


def _fox_kernel(lo_ref, q_ref, k_ref, v_ref, ncf_ref, o_ref, *, tq):
    qi = pl.program_id(2)
    first = lo_ref[(pl.program_id(0) * pl.num_programs(1) + pl.program_id(1)) * pl.num_programs(2) + qi]
    hd = FOX_HEAD_DIM
    heads = q_ref.shape[-1] // hd

    def block(kb, carry, masked):
        off = pl.multiple_of(kb * tq, tq)
        new = []
        for hh in range(heads):
            cols = slice(hh * hd, (hh + 1) * hd)
            m, l, acc = carry[hh]
            q = q_ref[0, :, cols]
            k = k_ref[0, pl.ds(off, tq), cols]
            v = v_ref[0, pl.ds(off, tq), cols]
            s = lax.dot_general(q, k, (((1,), (1,)), ((), ())), preferred_element_type=F32)
            s = s + ncf_ref[0, hh:hh + 1, pl.ds(off, tq)]
            if masked:
                row = lax.broadcasted_iota(jnp.int32, (tq, tq), 0)
                col = lax.broadcasted_iota(jnp.int32, (tq, tq), 1)
                s = jnp.where(row >= col, s, -jnp.inf)
            m_new = jnp.maximum(m, jnp.max(s, axis=-1, keepdims=True))
            alpha = jnp.exp2(m - m_new)
            p = jnp.exp2(s - m_new)
            l = alpha * l + jnp.sum(p, axis=-1, keepdims=True)
            acc = alpha * acc + jnp.dot(p.astype(BF16), v, preferred_element_type=F32)
            new.append((m_new, l, acc))
        return tuple(new)

    init = tuple((jnp.full((tq, 1), -jnp.inf, F32), jnp.zeros((tq, 1), F32), jnp.zeros((tq, hd), F32))
                 for _ in range(heads))
    carry = lax.fori_loop(first, qi, lambda kb, c: block(kb, c, False), init)
    final = block(qi, carry, True)
    for hh in range(heads):
        _, l, acc = final[hh]
        o_ref[0, :, hh * hd:(hh + 1) * hd] = (acc / l).astype(o_ref.dtype)


FOX_PRUNE_MARGIN = 150.0


def _fox_first_block(neg_cum_f, qk_bound, *, tq, heads):
    n_seq, _, seq = neg_cum_f.shape[0], neg_cum_f.shape[1], neg_cum_f.shape[2]
    nq = seq // tq
    blk = neg_cum_f.reshape(n_seq, heads, nq, tq)
    blk_max = lax.cummax(jnp.max(blk, axis=-1), axis=2)
    row_min = jnp.min(blk, axis=-1)
    thresh = row_min - (2.0 * qk_bound + FOX_PRUNE_MARGIN)
    skippable = blk_max[:, :, None, :] < thresh[:, :, :, None]
    first = jnp.sum(skippable.astype(jnp.int32), axis=-1)
    first = jnp.min(first, axis=1)
    first = jnp.minimum(first, jnp.arange(nq, dtype=jnp.int32)[None, :])
    return first.reshape(-1).astype(jnp.int32)


def _fox_attention(proj3d, neg_cum_f, qk_bound, *, tq, heads):
    bsz, seq, _ = proj3d.shape
    hw = heads * FOX_HEAD_DIM
    groups = FOX_HEADS // heads
    first_block = _fox_first_block(neg_cum_f, qk_bound, tq=tq, heads=heads)
    return pl.pallas_call(
        functools.partial(_fox_kernel, tq=tq),
        grid_spec=pltpu.PrefetchScalarGridSpec(
            num_scalar_prefetch=1,
            grid=(bsz, groups, seq // tq),
            in_specs=[
                pl.BlockSpec((1, tq, hw), lambda b, g, i, lo: (b, i, COL_FQ // hw + g)),
                pl.BlockSpec((1, seq, hw), lambda b, g, i, lo: (b, 0, COL_FK // hw + g)),
                pl.BlockSpec((1, seq, hw), lambda b, g, i, lo: (b, 0, COL_FV // hw + g)),
                pl.BlockSpec((1, heads, seq), lambda b, g, i, lo: (b * groups + g, 0, 0)),
            ],
            out_specs=pl.BlockSpec((1, tq, hw), lambda b, g, i, lo: (b, i, g)),
        ),
        out_shape=jax.ShapeDtypeStruct((bsz, seq, FOX_W), BF16),
        compiler_params=_cparams(3),
        name="fox_attn",
    )(first_block, proj3d, proj3d, proj3d, neg_cum_f)


def _mlstm_kernel(q_ref, k_ref, v_ref, mo_ref, gc_ref, gr_ref, gout_ref, o_ref,
                  c_ref, n_ref, m_ref):
    c_idx = pl.program_id(1)
    L = q_ref.shape[1]

    @pl.when(c_idx == 0)
    def _():
        c_ref[...] = jnp.zeros_like(c_ref)
        n_ref[...] = jnp.zeros_like(n_ref)
        m_ref[...] = jnp.zeros_like(m_ref)

    row = lax.broadcasted_iota(jnp.int32, (L, L), 0)
    col = lax.broadcasted_iota(jnp.int32, (L, L), 1)
    tril = col <= row

    for h in range(ML_HEADS):
        qs = slice(h * ML_QK_DIM, (h + 1) * ML_QK_DIM)
        vs = slice(h * ML_V_DIM, (h + 1) * ML_V_DIM)
        q = q_ref[0, :, qs]
        k = k_ref[0, :, qs]
        v = v_ref[0, :, vs]
        b_col = gc_ref[0, :, GCOL_MF + h:GCOL_MF + h + 1]
        i_col = gc_ref[0, :, GCOL_MI + h:GCOL_MI + h + 1]
        b_row = gr_ref[0, GCOL_MF + h:GCOL_MF + h + 1, :]
        i_row = gr_ref[0, GCOL_MI + h:GCOL_MI + h + 1, :]
        m_prev = m_ref[h, 0:1, 0:1]
        c_prev = c_ref[h]
        n_prev = n_ref[h]

        d_log = jnp.where(tril, b_col - b_row + i_row, -jnp.inf)
        g_inter = b_col + m_prev
        m_row = jnp.maximum(g_inter, jnp.max(d_log, axis=-1, keepdims=True))
        w_intra = jnp.exp(d_log - m_row)
        w_inter = jnp.exp(g_inter - m_row)
        qk = lax.dot_general(q, k, (((1,), (1,)), ((), ())), preferred_element_type=F32)
        scores = qk * w_intra
        num = (w_inter * jnp.dot(q, c_prev.astype(BF16), preferred_element_type=F32)
               + jnp.dot(scores.astype(BF16), v, preferred_element_type=F32))
        den = (w_inter * jnp.sum(q.astype(F32) * n_prev, axis=-1, keepdims=True)
               + jnp.sum(scores, axis=-1, keepdims=True))
        hh = num / jnp.maximum(jnp.abs(den), jnp.exp(-m_row))

        b_last = b_col[L - 1:L, :]
        a_log = b_last - b_col + i_col
        m_new = jnp.maximum(b_last + m_prev, jnp.max(a_log, axis=0, keepdims=True))
        decay = jnp.exp(b_last + m_prev - m_new)
        w_upd = jnp.exp(a_log - m_new)
        kw = k.astype(F32) * w_upd
        c_ref[h] = decay * c_prev + jnp.dot(kw.T.astype(BF16), v, preferred_element_type=F32)
        n_ref[h] = decay * n_prev + jnp.sum(kw, axis=0, keepdims=True)
        m_ref[h] = jnp.broadcast_to(m_new, m_ref.shape[1:])

        ms = jnp.mean(hh * hh, axis=-1, keepdims=True)
        y = (hh * lax.rsqrt(ms + EPS) * gout_ref[:, vs]
             * jax.nn.sigmoid(mo_ref[0, :, vs].astype(F32)))
        o_ref[0, :, vs] = y.astype(o_ref.dtype)


def _mlstm(proj3d, gate_cols, gate_rows, gout, *, chunk):
    bsz, seq, _ = proj3d.shape
    return pl.pallas_call(
        _mlstm_kernel,
        grid=(bsz, seq // chunk),
        in_specs=[
            pl.BlockSpec((1, chunk, ML_QK_W), lambda b, c: (b, c, COL_MQ // ML_QK_W)),
            pl.BlockSpec((1, chunk, ML_QK_W), lambda b, c: (b, c, COL_MK // ML_QK_W)),
            pl.BlockSpec((1, chunk, ML_V_W), lambda b, c: (b, c, COL_MV // ML_V_W)),
            pl.BlockSpec((1, chunk, ML_V_W), lambda b, c: (b, c, COL_MO // ML_V_W)),
            pl.BlockSpec((1, chunk, LANES), lambda b, c: (b, c, 0)),
            pl.BlockSpec((1, 16, chunk), lambda b, c: (b, 0, c)),
            pl.BlockSpec((1, ML_V_W), lambda b, c: (0, 0)),
        ],
        out_specs=pl.BlockSpec((1, chunk, ML_V_W), lambda b, c: (b, c, 0)),
        out_shape=jax.ShapeDtypeStruct((bsz, seq, ML_V_W), BF16),
        scratch_shapes=[
            pltpu.VMEM((ML_HEADS, ML_QK_DIM, ML_V_DIM), F32),
            pltpu.VMEM((ML_HEADS, 1, ML_QK_DIM), F32),
            pltpu.VMEM((ML_HEADS, 8, LANES), F32),
        ],
        compiler_params=_cparams(2),
        name="mlstm",
    )(proj3d, proj3d, proj3d, proj3d, gate_cols, gate_rows, gout)


def _merge_kernel(ya_ref, yb_ref, ga_ref, gb_ref, x_ref, wa_ref, wb_ref, wo_ref, g2_ref,
                  wr_ref, br_ref, h_ref, tp_ref, route_ref, cnt_ref, carry_ref):
    i = pl.program_id(0)
    tm = x_ref.shape[0]

    @pl.when(i == 0)
    def _():
        carry_ref[...] = jnp.zeros_like(carry_ref)

    a = jnp.dot(ya_ref[...], wa_ref[...], preferred_element_type=F32)
    b = jnp.dot(yb_ref[...], wb_ref[...], preferred_element_type=F32)
    merged = (jax.nn.sigmoid(ga_ref[...].astype(F32)) * a
              + jax.nn.sigmoid(gb_ref[...].astype(F32)) * b)
    h = x_ref[...] + jnp.dot(merged.astype(BF16), wo_ref[...], preferred_element_type=F32)
    h_ref[...] = h
    t = _rms_norm_rows(h, g2_ref[...])
    tp_ref[...] = _pack_rows(t)

    t_hi = t.astype(BF16)
    t_lo = (t - t_hi.astype(F32)).astype(BF16)
    w_r = wr_ref[...]
    w_hi = w_r.astype(BF16)
    w_lo = (w_r - w_hi.astype(F32)).astype(BF16)
    logits = (jnp.dot(t_hi, w_hi, preferred_element_type=F32)
              + jnp.dot(t_lo, w_hi, preferred_element_type=F32)
              + jnp.dot(t_hi, w_lo, preferred_element_type=F32)
              + br_ref[...])
    lane = lax.broadcasted_iota(jnp.int32, (tm, LANES), 1).astype(F32)
    lg = logits
    sels, vals, idxs = [], [], []
    for _ in range(TOP_K):
        mx = jnp.max(lg, axis=-1, keepdims=True)
        ik = jnp.min(jnp.where(lg == mx, lane, float(LANES)), axis=-1, keepdims=True)
        sel = lane == ik
        sels.append(sel)
        vals.append(mx)
        idxs.append(ik)
        lg = jnp.where(sel, -jnp.inf, lg)
    exps = [jnp.exp(v - vals[0]) for v in vals]
    den = exps[0] + exps[1] + exps[2] + exps[3]
    mask = jnp.zeros((tm, LANES), F32)
    for sel in sels:
        mask = mask + jnp.where(sel, 1.0, 0.0)
    row = lax.broadcasted_iota(jnp.int32, (tm, tm), 0)
    col = lax.broadcasted_iota(jnp.int32, (tm, tm), 1)
    strict = jnp.where(col < row, 1.0, 0.0).astype(BF16)
    ranks = jnp.dot(strict, mask.astype(BF16), preferred_element_type=F32) + carry_ref[...]
    slab = jnp.zeros((tm, LANES), F32)
    for kk in range(TOP_K):
        rank_k = jnp.sum(jnp.where(sels[kk], ranks, 0.0), axis=-1, keepdims=True)
        slab = jnp.where(lane == float(kk), idxs[kk], slab)
        slab = jnp.where(lane == float(TOP_K + kk), rank_k, slab)
        slab = jnp.where(lane == float(2 * TOP_K + kk), exps[kk] / den, slab)
    route_ref[...] = slab
    new_carry = carry_ref[...] + jnp.sum(mask, axis=0, keepdims=True)
    carry_ref[...] = new_carry
    cnt_ref[...] = new_carry


def _merge(y_a, y_b, proj, x2d, w_a, w_b, w_o, g2, w_r, b_r, *, tm):
    t_rows = x2d.shape[0]
    const = lambda shape: pl.BlockSpec(shape, lambda i: (0, 0), pipeline_mode=pl.Buffered(1))
    return pl.pallas_call(
        _merge_kernel,
        grid=(t_rows // tm,),
        in_specs=[
            pl.BlockSpec((tm, FOX_W), lambda i: (i, 0)),
            pl.BlockSpec((tm, ML_V_W), lambda i: (i, 0)),
            pl.BlockSpec((tm, D_MODEL), lambda i: (i, COL_GA // D_MODEL)),
            pl.BlockSpec((tm, D_MODEL), lambda i: (i, COL_GB // D_MODEL)),
            pl.BlockSpec((tm, D_MODEL), lambda i: (i, 0)),
            const((FOX_W, D_MODEL)),
            const((ML_V_W, D_MODEL)),
            const((D_MODEL, D_MODEL)),
            const((1, D_MODEL)),
            const((D_MODEL, LANES)),
            const((1, LANES)),
        ],
        out_specs=[
            pl.BlockSpec((tm, D_MODEL), lambda i: (i, 0)),
            pl.BlockSpec((tm, HALF), lambda i: (i, 0)),
            pl.BlockSpec((tm, LANES), lambda i: (i, 0)),
            pl.BlockSpec((1, LANES), lambda i: (0, 0)),
        ],
        out_shape=[
            jax.ShapeDtypeStruct((t_rows, D_MODEL), F32),
            jax.ShapeDtypeStruct((t_rows, HALF), jnp.uint32),
            jax.ShapeDtypeStruct((t_rows, LANES), F32),
            jax.ShapeDtypeStruct((1, LANES), F32),
        ],
        scratch_shapes=[pltpu.VMEM((1, LANES), F32)],
        compiler_params=_cparams(1),
        name="merge_router",
    )(y_a, y_b, proj, proj, x2d, w_a, w_b, w_o, g2, w_r, b_r)


def _dispatch_kernel(pos_ref, zt_ref, tp_ref, xs_ref, stage_ref, zero_ref, sem, zsem, *, tr, tile):
    i = pl.program_id(0)
    n = pl.num_programs(0)

    @pl.when(i == 0)
    def _():
        zero_ref[...] = jnp.zeros_like(zero_ref)

        def zero_copy(z):
            start = pl.multiple_of(jnp.maximum(zt_ref[z], 0) * tile, tile)
            return pltpu.make_async_copy(zero_ref, xs_ref.at[pl.ds(start, tile), :], zsem)

        def start_body(z, carry):
            @pl.when(zt_ref[z] >= 0)
            def _():
                zero_copy(z).start()
            return carry
        lax.fori_loop(0, zt_ref.shape[0], start_body, 0)

        def wait_body(z, carry):
            @pl.when(zt_ref[z] >= 0)
            def _():
                zero_copy(z).wait()
            return carry
        lax.fori_loop(0, zt_ref.shape[0], wait_body, 0)

    def row_copy(s, r, p):
        return pltpu.make_async_copy(stage_ref.at[s, pl.ds(r, 1), :], xs_ref.at[pl.ds(p, 1), :], sem.at[s])

    slot = i % 2
    stage_ref[slot] = tp_ref[...]
    base = i * (tr * TOP_K)

    def issue(r, carry):
        for kk in range(TOP_K):
            row_copy(slot, r, pos_ref[base + r * TOP_K + kk]).start()
        return carry
    lax.fori_loop(0, tr, issue, 0, unroll=4)

    def drain(s):
        def body(r, carry):
            for kk in range(TOP_K):
                row_copy(s, r, 0).wait()
            return carry
        lax.fori_loop(0, tr, body, 0)

    @pl.when(i > 0)
    def _():
        drain(1 - slot)

    @pl.when(i == n - 1)
    def _():
        drain(slot)


def _dispatch(pos_flat, zero_tiles, tpk, *, n_rows, tr, tile):
    t_rows = tpk.shape[0]
    return pl.pallas_call(
        functools.partial(_dispatch_kernel, tr=tr, tile=tile),
        grid_spec=pltpu.PrefetchScalarGridSpec(
            num_scalar_prefetch=2,
            grid=(t_rows // tr,),
            in_specs=[pl.BlockSpec((tr, HALF), lambda i, pos, zt: (i, 0))],
            out_specs=pl.BlockSpec(memory_space=pl.ANY),
            scratch_shapes=[
                pltpu.VMEM((2, tr, HALF), jnp.uint32),
                pltpu.VMEM((tile, HALF), jnp.uint32),
                pltpu.SemaphoreType.DMA((2,)),
                pltpu.SemaphoreType.DMA(()),
            ],
        ),
        out_shape=jax.ShapeDtypeStruct((n_rows, HALF), jnp.uint32),
        compiler_params=_cparams(1),
        name="dispatch",
    )(pos_flat, zero_tiles, tpk)


def _experts_kernel(te_ref, nv_ref, nu_ref, xs_ref, wg_ref, bg_ref, wu_ref, bu_ref, wd_ref, bd_ref,
                    o_ref, xb_ref, acc_ref, *, sub):
    del te_ref, nu_ref
    i = pl.program_id(0)
    j = pl.program_id(1)
    n_ff = pl.num_programs(1)
    tm = xs_ref.shape[0]
    valid = nv_ref[i]

    @pl.when(j == 0)
    def _():
        lo, hi = _unpack_rows(xs_ref[...])
        xb_ref[:, :HALF] = lo.astype(BF16)
        xb_ref[:, HALF:] = hi.astype(BF16)
        acc_ref[...] = jnp.broadcast_to(bd_ref[0], acc_ref.shape)

    def ffn_chunk(n_rows):
        xb = xb_ref[0:n_rows, :]
        g = jnp.dot(xb, wg_ref[0].astype(BF16), preferred_element_type=F32) + bg_ref[0]
        u = jnp.dot(xb, wu_ref[0].astype(BF16), preferred_element_type=F32) + bu_ref[0]
        g = jnp.minimum(g, SWIGLU_LIMIT)
        u = jnp.clip(u, -SWIGLU_LIMIT, SWIGLU_LIMIT)
        act = (u + 1.0) * g * jax.nn.sigmoid(SWIGLU_ALPHA * g)
        acc_ref[0:n_rows, :] += jnp.dot(act.astype(BF16), wd_ref[0].astype(BF16),
                                        preferred_element_type=F32)

    n_sub = tm // sub
    for s in range(n_sub):
        lo_rows, hi_rows = s * sub, (s + 1) * sub
        upper = (valid <= hi_rows) if s + 1 < n_sub else True

        @pl.when((valid > lo_rows) & upper)
        def _():
            ffn_chunk(hi_rows)

    @pl.when(j == n_ff - 1)
    def _():
        o_ref[...] = _pack_rows(acc_ref[...])


def _experts(tile_expert, tile_valid, n_used, xs, w_gate, b_gate, w_up, b_up, w_down, b_down,
             *, tm, tf, sub):
    n_tiles = xs.shape[0] // tm
    n_ff = D_FF // tf

    def tile_i(i, nu):
        return jnp.maximum(jnp.minimum(i, nu[0] - 1), 0)

    def ff_j(i, j, nu):
        return jnp.where(i < nu[0], j, n_ff - 1)

    return pl.pallas_call(
        functools.partial(_experts_kernel, sub=sub),
        grid_spec=pltpu.PrefetchScalarGridSpec(
            num_scalar_prefetch=3,
            grid=(n_tiles, n_ff),
            in_specs=[
                pl.BlockSpec((tm, HALF), lambda i, j, te, nv, nu: (tile_i(i, nu), 0)),
                pl.BlockSpec((1, D_MODEL, tf), lambda i, j, te, nv, nu: (te[i], 0, ff_j(i, j, nu))),
                pl.BlockSpec((1, 1, tf), lambda i, j, te, nv, nu: (te[i], 0, ff_j(i, j, nu))),
                pl.BlockSpec((1, D_MODEL, tf), lambda i, j, te, nv, nu: (te[i], 0, ff_j(i, j, nu))),
                pl.BlockSpec((1, 1, tf), lambda i, j, te, nv, nu: (te[i], 0, ff_j(i, j, nu))),
                pl.BlockSpec((1, tf, D_MODEL), lambda i, j, te, nv, nu: (te[i], ff_j(i, j, nu), 0)),
                pl.BlockSpec((1, 1, D_MODEL), lambda i, j, te, nv, nu: (te[i], 0, 0)),
            ],
            out_specs=pl.BlockSpec((tm, HALF), lambda i, j, te, nv, nu: (i, 0)),
            scratch_shapes=[
                pltpu.VMEM((tm, D_MODEL), BF16),
                pltpu.VMEM((tm, D_MODEL), F32),
            ],
        ),
        out_shape=jax.ShapeDtypeStruct(xs.shape, jnp.uint32),
        compiler_params=_cparams(2),
        name="experts",
    )(tile_expert, tile_valid, n_used, xs, w_gate, b_gate, w_up, b_up, w_down, b_down)


def _combine_kernel(pos_ref, h_ref, w_ref, ys_ref, o_ref, buf_ref, sem):
    i = pl.program_id(0)
    n = pl.num_programs(0)
    tc = h_ref.shape[0]

    def row_copy(p, slot, kk, r):
        return pltpu.make_async_copy(ys_ref.at[pl.ds(p, 1), :],
                                     buf_ref.at[slot, kk, pl.ds(r, 1), :], sem.at[slot])

    def issue(step, slot):
        base = step * (tc * TOP_K)

        def body(r, carry):
            for kk in range(TOP_K):
                row_copy(pos_ref[base + r * TOP_K + kk], slot, kk, r).start()
            return carry
        lax.fori_loop(0, tc, body, 0, unroll=4)

    @pl.when(i == 0)
    def _():
        issue(0, 0)

    @pl.when(i + 1 < n)
    def _():
        issue(i + 1, (i + 1) % 2)

    slot = i % 2

    def drain(r, carry):
        for kk in range(TOP_K):
            row_copy(0, slot, kk, r).wait()
        return carry
    lax.fori_loop(0, tc, drain, 0)

    w = w_ref[...]
    acc_lo = h_ref[:, :HALF]
    acc_hi = h_ref[:, HALF:]
    for kk in range(TOP_K):
        lo, hi = _unpack_rows(buf_ref[slot, kk])
        acc_lo = acc_lo + w[:, kk:kk + 1] * lo
        acc_hi = acc_hi + w[:, kk:kk + 1] * hi
    o_ref[:, :HALF] = acc_lo
    o_ref[:, HALF:] = acc_hi


def _combine(pos_flat, h2d, w_top, ys, *, tc):
    t_rows = h2d.shape[0]
    return pl.pallas_call(
        _combine_kernel,
        grid_spec=pltpu.PrefetchScalarGridSpec(
            num_scalar_prefetch=1,
            grid=(t_rows // tc,),
            in_specs=[
                pl.BlockSpec((tc, D_MODEL), lambda i, pos: (i, 0)),
                pl.BlockSpec((tc, TOP_K), lambda i, pos: (i, 0)),
                pl.BlockSpec(memory_space=pl.ANY),
            ],
            out_specs=pl.BlockSpec((tc, D_MODEL), lambda i, pos: (i, 0)),
            scratch_shapes=[
                pltpu.VMEM((2, TOP_K, tc, HALF), jnp.uint32),
                pltpu.SemaphoreType.DMA((2,)),
            ],
        ),
        out_shape=jax.ShapeDtypeStruct(h2d.shape, F32),
        compiler_params=_cparams(1),
        name="combine",
    )(pos_flat, h2d, w_top, ys)


def _pick(n, pref):
    t = min(n, pref)
    assert n % t == 0, (n, t)
    return t


def kernel(x, norm1_g, w_in, fox_f_bias, q_norm_g, k_norm_g, ml_i_bias, ml_f_bias, ml_out_norm_g,
           w_branch_a, w_branch_b, w_out, norm2_g, w_router, b_router,
           w_gate, b_gate, w_up, b_up, w_down, b_down):
    bsz, seq, d_model = x.shape
    assert d_model == D_MODEL and norm1_g.shape[0] == 1, "single-layer block of width 2048"
    t_rows = bsz * seq
    x2d = x.reshape(t_rows, D_MODEL).astype(F32)

    offs = [0]
    for wdt in IN_WIDTHS:
        offs.append(offs[-1] + wdt)
    seg = lambda n: w_in[0][:, offs[n]:offs[n + 1]]
    w_main = jnp.concatenate([seg(0), seg(1), seg(2), seg(4), seg(5), seg(6), seg(9), seg(10), seg(11)],
                             axis=1).astype(BF16)
    w_gl = jnp.concatenate([seg(3), seg(7), seg(8),
                            jnp.zeros((D_MODEL, LANES - FOX_HEADS - 2 * ML_HEADS), F32)], axis=1).astype(BF16)
    colscale = jnp.ones((PROJ_W,), F32)
    colscale = colscale.at[COL_FQ:COL_FQ + FOX_W].set(
        jnp.tile(q_norm_g[0].astype(F32), FOX_HEADS) * (FOX_HEAD_DIM ** -0.5 * LOG2_E))
    colscale = colscale.at[COL_FK:COL_FK + FOX_W].set(jnp.tile(k_norm_g[0].astype(F32), FOX_HEADS))
    colscale = colscale.at[COL_MK:COL_MK + ML_QK_W].set(ML_QK_DIM ** -0.5)
    colscale = colscale.reshape(1, PROJ_W)
    gate_bias = jnp.concatenate([fox_f_bias[0], ml_i_bias[0], ml_f_bias[0],
                                 jnp.zeros((LANES - FOX_HEADS - 2 * ML_HEADS,), F32)]).astype(F32).reshape(1, LANES)
    g2 = norm2_g.astype(F32).reshape(1, D_MODEL)

    proj, gates = _in_proj(x2d, norm1_g.astype(F32).reshape(1, D_MODEL), w_main, w_gl, colscale,
                           tm=_pick(t_rows, 1024), tn=512)
    chunk = _pick(seq, 256)
    gp = _gate_prep(gates.reshape(bsz, seq, LANES), gate_bias, ts=_pick(seq, 512), chunk=chunk)
    gp_rows = jnp.transpose(gp[:, :, :16], (0, 2, 1))
    fox_group = 2
    neg_cum_f = (-LOG2_E * gp_rows[:, GCOL_FF:GCOL_FF + FOX_HEADS, :]).reshape(
        bsz * FOX_HEADS // fox_group, fox_group, seq)
    proj3d = proj.reshape(bsz, seq, PROJ_W)
    qk_bound = (1.02 * FOX_HEAD_DIM * (FOX_HEAD_DIM ** -0.5 * LOG2_E)
                * jnp.max(jnp.abs(q_norm_g[0].astype(F32))) * jnp.max(jnp.abs(k_norm_g[0].astype(F32))))
    y_a = _fox_attention(proj3d, neg_cum_f, qk_bound, tq=_pick(seq, 512), heads=fox_group)
    y_b = _mlstm(proj3d, gp, gp_rows, ml_out_norm_g.astype(F32).reshape(1, ML_V_W), chunk=chunk)

    w_r = jnp.concatenate([w_router[0].astype(F32), jnp.zeros((D_MODEL, LANES - N_EXPERTS), F32)], axis=1)
    b_r = jnp.concatenate([b_router[0].astype(F32), jnp.full((LANES - N_EXPERTS,), -jnp.inf, F32)]).reshape(1, LANES)
    h2d, tpk, route, counts = _merge(
        y_a.reshape(t_rows, FOX_W), y_b.reshape(t_rows, ML_V_W), proj, x2d,
        w_branch_a[0].astype(BF16), w_branch_b[0].astype(BF16), w_out[0].astype(BF16),
        g2, w_r, b_r, tm=_pick(t_rows, 512))

    tm_e = _pick(t_rows * TOP_K, 1024)
    n_tiles = (t_rows * TOP_K) // tm_e + N_EXPERTS
    top_idx = route[:, 0:TOP_K].astype(jnp.int32)
    top_rank = route[:, TOP_K:2 * TOP_K].astype(jnp.int32)
    top_w = route[:, 2 * TOP_K:3 * TOP_K]
    cnt = counts[0, :N_EXPERTS].astype(jnp.int32)
    tiles_per_e = (cnt + tm_e - 1) // tm_e
    tile_end = jnp.cumsum(tiles_per_e)
    tile_begin = tile_end - tiles_per_e
    pos_flat = ((tile_begin * tm_e)[top_idx] + top_rank).reshape(-1)
    n_used = tile_end[-1:].astype(jnp.int32)
    tile_ids = jnp.arange(n_tiles, dtype=jnp.int32)
    used_ids = jnp.minimum(tile_ids, n_used[0] - 1)
    tile_expert = jnp.sum((used_ids[:, None] >= tile_end[None, :]).astype(jnp.int32), axis=1)
    tile_valid = jnp.clip(cnt[tile_expert] - (used_ids - tile_begin[tile_expert]) * tm_e, 0, tm_e)
    tile_valid = jnp.where(tile_ids < n_used[0], tile_valid, 0).astype(jnp.int32)

    last_tile = jnp.where(tiles_per_e > 0, tile_end - 1, -1)
    tail_ids = tile_ids[n_tiles - N_EXPERTS:]
    zero_tiles = jnp.concatenate([last_tile, jnp.where(tail_ids >= n_used[0], tail_ids, -1)]).astype(jnp.int32)
    xs = _dispatch(pos_flat, zero_tiles, tpk, n_rows=n_tiles * tm_e, tr=_pick(t_rows, 256), tile=tm_e)
    ys = _experts(tile_expert, tile_valid, n_used, xs,
                  w_gate[0], b_gate[0].reshape(N_EXPERTS, 1, D_FF),
                  w_up[0], b_up[0].reshape(N_EXPERTS, 1, D_FF),
                  w_down[0], b_down[0].reshape(N_EXPERTS, 1, D_MODEL),
                  tm=tm_e, tf=256, sub=_pick(tm_e, 256))
    out = _combine(pos_flat, h2d, top_w, ys, tc=_pick(t_rows, 128))
    return out.reshape(bsz, seq, D_MODEL).astype(x.dtype)
```

```python
import functools

import jax
import jax.numpy as jnp
from jax import lax
from jax.experimental import pallas as pl
from jax.experimental.pallas import tpu as pltpu

F32 = jnp.float32
BF16 = jnp.bfloat16

D_MODEL = 2048
FOX_HEADS = 8
FOX_HEAD_DIM = 128
FOX_W = FOX_HEADS * FOX_HEAD_DIM
ML_HEADS = 4
ML_QK_DIM = 128
ML_V_DIM = 256
ML_QK_W = ML_HEADS * ML_QK_DIM
ML_V_W = ML_HEADS * ML_V_DIM
IGATE_CAP = 15.0
N_EXPERTS = 32
TOP_K = 4
D_FF = 2048
SWIGLU_LIMIT = 7.0
SWIGLU_ALPHA = 1.702
EPS = 1e-5

IN_WIDTHS = (FOX_W, FOX_W, FOX_W, FOX_HEADS,
             ML_QK_W, ML_QK_W, ML_V_W, ML_HEADS, ML_HEADS, ML_V_W,
             D_MODEL, D_MODEL)

LANES = 128
VMEM_LIMIT = 56 * 1024 * 1024

COL_FQ, COL_FK, COL_FV = 0, FOX_W, 2 * FOX_W
COL_MQ = 3 * FOX_W
COL_MK = COL_MQ + ML_QK_W
COL_MV = COL_MK + ML_QK_W
COL_MO = COL_MV + ML_V_W
COL_GA = COL_MO + ML_V_W
COL_GB = COL_GA + D_MODEL
PROJ_W = COL_GB + D_MODEL
GCOL_FF, GCOL_MI, GCOL_MF = 0, FOX_HEADS, FOX_HEADS + ML_HEADS

HALF = D_MODEL // 2
LOG2_E = 1.4426950408889634


def _cparams(n_axes, vmem=VMEM_LIMIT):
    return pltpu.CompilerParams(dimension_semantics=("arbitrary",) * n_axes,
                                vmem_limit_bytes=vmem)


def _log_sigmoid(x):
    return jnp.minimum(x, 0.0) - jnp.log1p(jnp.exp(-jnp.abs(x)))


def _pack_rows(x):
    lo = lax.bitcast_convert_type(x[:, :HALF].astype(BF16).astype(F32), jnp.uint32)
    hi = lax.bitcast_convert_type(x[:, HALF:].astype(BF16).astype(F32), jnp.uint32)
    return hi | (lo >> 16)


def _unpack_rows(p):
    lo = lax.bitcast_convert_type(p << 16, F32)
    hi = lax.bitcast_convert_type(p & jnp.uint32(0xFFFF0000), F32)
    return lo, hi


def _rms_norm_rows(h, g):
    ms = jnp.mean(h * h, axis=-1, keepdims=True)
    return h * lax.rsqrt(ms + EPS) * g


def _in_proj_kernel(x_ref, g1_ref, w_ref, wg_ref, cs_ref, o_ref, gate_ref, xn_ref,
                    *, n_norm_blocks, row_chunk):
    j = pl.program_id(1)
    tm = x_ref.shape[0]
    tn = w_ref.shape[1]

    @pl.when(j == 0)
    def _():
        def body(c, carry):
            r0 = pl.multiple_of(c * row_chunk, row_chunk)
            x = x_ref[pl.ds(r0, row_chunk), :]
            ms = jnp.mean(x * x, axis=-1, keepdims=True)
            xn_ref[pl.ds(r0, row_chunk), :] = (x * lax.rsqrt(ms + EPS) * g1_ref[...]).astype(BF16)
            return carry
        lax.fori_loop(0, tm // row_chunk, body, 0)
        gate_ref[...] = jnp.dot(xn_ref[...], wg_ref[...], preferred_element_type=F32)

    @pl.when(j < n_norm_blocks)
    def _():
        acc = jnp.dot(xn_ref[...], w_ref[...], preferred_element_type=F32)
        cs = cs_ref[...]
        for s in range(tn // LANES):
            a = acc[:, s * LANES:(s + 1) * LANES]
            ms = jnp.mean(a * a, axis=-1, keepdims=True)
            y = a * lax.rsqrt(ms + EPS) * cs[:, s * LANES:(s + 1) * LANES]
            o_ref[:, s * LANES:(s + 1) * LANES] = y.astype(o_ref.dtype)

    @pl.when(j >= n_norm_blocks)
    def _():
        half = tn // 2
        for c0 in (0, half):
            acc = jnp.dot(xn_ref[...], w_ref[:, c0:c0 + half], preferred_element_type=F32)
            o_ref[:, c0:c0 + half] = (acc * cs_ref[:, c0:c0 + half]).astype(o_ref.dtype)


def _in_proj(x2d, g1, w_main, w_gate, colscale, *, tm, tn):
    t_rows = x2d.shape[0]
    grid = (t_rows // tm, PROJ_W // tn)
    kern = functools.partial(_in_proj_kernel, n_norm_blocks=(2 * FOX_W) // tn, row_chunk=128)
    return pl.pallas_call(
        kern,
        grid=grid,
        in_specs=[
            pl.BlockSpec((tm, D_MODEL), lambda i, j: (i, 0)),
            pl.BlockSpec((1, D_MODEL), lambda i, j: (0, 0)),
            pl.BlockSpec((D_MODEL, tn), lambda i, j: (0, j)),
            pl.BlockSpec((D_MODEL, LANES), lambda i, j: (0, 0)),
            pl.BlockSpec((1, tn), lambda i, j: (0, j)),
        ],
        out_specs=[
            pl.BlockSpec((tm, tn), lambda i, j: (i, j)),
            pl.BlockSpec((tm, LANES), lambda i, j: (i, 0)),
        ],
        out_shape=[
            jax.ShapeDtypeStruct((t_rows, PROJ_W), BF16),
            jax.ShapeDtypeStruct((t_rows, LANES), F32),
        ],
        scratch_shapes=[pltpu.VMEM((tm, D_MODEL), BF16)],
        compiler_params=_cparams(2),
        name="in_proj",
    )(x2d, g1, w_main, w_gate, colscale)


def _gate_prep_kernel(g_ref, b_ref, o_ref, carry_ref, *, chunk):
    s = pl.program_id(1)
    ts = g_ref.shape[1]

    @pl.when(s == 0)
    def _():
        carry_ref[...] = jnp.zeros_like(carry_ref)

    z = g_ref[0] + b_ref[...]
    log_f = _log_sigmoid(z)
    i_pre = IGATE_CAP * jnp.tanh(z / IGATE_CAP)
    row = lax.broadcasted_iota(jnp.int32, (ts, ts), 0)
    col = lax.broadcasted_iota(jnp.int32, (ts, ts), 1)
    tril = (col <= row)
    same_chunk = (row // chunk) == (col // chunk)
    tril_f = jnp.where(tril, 1.0, 0.0).astype(F32)
    tril_c = jnp.where(tril & same_chunk, 1.0, 0.0).astype(F32)
    run_sum = jnp.dot(tril_f, log_f, preferred_element_type=F32,
                      precision=lax.Precision.HIGHEST) + carry_ref[...]
    chunk_sum = jnp.dot(tril_c, log_f, preferred_element_type=F32,
                        precision=lax.Precision.HIGHEST)
    carry_ref[...] = run_sum[ts - 1:ts, :]
    lane = lax.broadcasted_iota(jnp.int32, (ts, LANES), 1)
    o_ref[0] = jnp.where(lane < GCOL_MI, run_sum, jnp.where(lane < GCOL_MF, i_pre, chunk_sum))


def _gate_prep(gates3d, bias, *, ts, chunk):
    bsz, seq, _ = gates3d.shape
    return pl.pallas_call(
        functools.partial(_gate_prep_kernel, chunk=chunk),
        grid=(bsz, seq // ts),
        in_specs=[
            pl.BlockSpec((1, ts, LANES), lambda b, s: (b, s, 0)),
            pl.BlockSpec((1, LANES), lambda b, s: (0, 0)),
        ],
        out_specs=pl.BlockSpec((1, ts, LANES), lambda b, s: (b, s, 0)),
        out_shape=jax.ShapeDtypeStruct((bsz, seq, LANES), F32),
        scratch_shapes=[pltpu.VMEM((1, LANES), F32)],
        compiler_params=_cparams(2),
        name="gate_prep",
    )(gates3d, bias)


def _fox_kernel(lo_ref, q_ref, k_ref, v_ref, ncf_ref, o_ref, *, tq):
    qi = pl.program_id(2)
    first = lo_ref[(pl.program_id(0) * pl.num_programs(1) + pl.program_id(1)) * pl.num_programs(2) + qi]
    hd = FOX_HEAD_DIM
    heads = q_ref.shape[-1] // hd

    def block(kb, carry, masked):
        off = pl.multiple_of(kb * tq, tq)
        new = []
        for hh in range(heads):
            cols = slice(hh * hd, (hh + 1) * hd)
            m, l, acc = carry[hh]
            q = q_ref[0, :, cols]
            k = k_ref[0, pl.ds(off, tq), cols]
            v = v_ref[0, pl.ds(off, tq), cols]
            s = lax.dot_general(q, k, (((1,), (1,)), ((), ())), preferred_element_type=F32)
            s = s + ncf_ref[0, hh:hh + 1, pl.ds(off, tq)]
            if masked:
                row = lax.broadcasted_iota(jnp.int32, (tq, tq), 0)
                col = lax.broadcasted_iota(jnp.int32, (tq, tq), 1)
                s = jnp.where(row >= col, s, -jnp.inf)
            m_new = jnp.maximum(m, jnp.max(s, axis=-1, keepdims=True))
            alpha = jnp.exp2(m - m_new)
            p = jnp.exp2(s - m_new)
            l = alpha * l + jnp.sum(p, axis=-1, keepdims=True)
            acc = alpha * acc + jnp.dot(p.astype(BF16), v, preferred_element_type=F32)
            new.append((m_new, l, acc))
        return tuple(new)

    init = tuple((jnp.full((tq, 1), -jnp.inf, F32), jnp.zeros((tq, 1), F32), jnp.zeros((tq, hd), F32))
                 for _ in range(heads))
    carry = lax.fori_loop(first, qi, lambda kb, c: block(kb, c, False), init)
    final = block(qi, carry, True)
    for hh in range(heads):
        _, l, acc = final[hh]
        o_ref[0, :, hh * hd:(hh + 1) * hd] = (acc / l).astype(o_ref.dtype)


FOX_PRUNE_MARGIN = 150.0


def _fox_first_block(neg_cum_f, qk_bound, *, tq, heads):
    n_seq, _, seq = neg_cum_f.shape[0], neg_cum_f.shape[1], neg_cum_f.shape[2]
    nq = seq // tq
    blk = neg_cum_f.reshape(n_seq, heads, nq, tq)
    blk_max = lax.cummax(jnp.max(blk, axis=-1), axis=2)
    row_min = jnp.min(blk, axis=-1)
    thresh = row_min - (2.0 * qk_bound + FOX_PRUNE_MARGIN)
    skippable = blk_max[:, :, None, :] < thresh[:, :, :, None]
    first = jnp.sum(skippable.astype(jnp.int32), axis=-1)
    first = jnp.min(first, axis=1)
    first = jnp.minimum(first, jnp.arange(nq, dtype=jnp.int32)[None, :])
    return first.reshape(-1).astype(jnp.int32)


def _fox_attention(proj3d, neg_cum_f, qk_bound, *, tq, heads):
    bsz, seq, _ = proj3d.shape
    hw = heads * FOX_HEAD_DIM
    groups = FOX_HEADS // heads
    first_block = _fox_first_block(neg_cum_f, qk_bound, tq=tq, heads=heads)
    return pl.pallas_call(
        functools.partial(_fox_kernel, tq=tq),
        grid_spec=pltpu.PrefetchScalarGridSpec(
            num_scalar_prefetch=1,
            grid=(bsz, groups, seq // tq),
            in_specs=[
                pl.BlockSpec((1, tq, hw), lambda b, g, i, lo: (b, i, COL_FQ // hw + g)),
                pl.BlockSpec((1, seq, hw), lambda b, g, i, lo: (b, 0, COL_FK // hw + g)),
                pl.BlockSpec((1, seq, hw), lambda b, g, i, lo: (b, 0, COL_FV // hw + g)),
                pl.BlockSpec((1, heads, seq), lambda b, g, i, lo: (b * groups + g, 0, 0)),
            ],
            out_specs=pl.BlockSpec((1, tq, hw), lambda b, g, i, lo: (b, i, g)),
        ),
        out_shape=jax.ShapeDtypeStruct((bsz, seq, FOX_W), BF16),
        compiler_params=_cparams(3),
        name="fox_attn",
    )(first_block, proj3d, proj3d, proj3d, neg_cum_f)


def _mlstm_kernel(q_ref, k_ref, v_ref, mo_ref, gc_ref, gr_ref, gout_ref, o_ref,
                  c_ref, n_ref, m_ref):
    c_idx = pl.program_id(1)
    L = q_ref.shape[1]

    @pl.when(c_idx == 0)
    def _():
        c_ref[...] = jnp.zeros_like(c_ref)
        n_ref[...] = jnp.zeros_like(n_ref)
        m_ref[...] = jnp.zeros_like(m_ref)

    row = lax.broadcasted_iota(jnp.int32, (L, L), 0)
    col = lax.broadcasted_iota(jnp.int32, (L, L), 1)
    tril = col <= row

    for h in range(ML_HEADS):
        qs = slice(h * ML_QK_DIM, (h + 1) * ML_QK_DIM)
        vs = slice(h * ML_V_DIM, (h + 1) * ML_V_DIM)
        q = q_ref[0, :, qs]
        k = k_ref[0, :, qs]
        v = v_ref[0, :, vs]
        b_col = gc_ref[0, :, GCOL_MF + h:GCOL_MF + h + 1]
        i_col = gc_ref[0, :, GCOL_MI + h:GCOL_MI + h + 1]
        b_row = gr_ref[0, GCOL_MF + h:GCOL_MF + h + 1, :]
        i_row = gr_ref[0, GCOL_MI + h:GCOL_MI + h + 1, :]
        m_prev = m_ref[h, 0:1, 0:1]
        c_prev = c_ref[h]
        n_prev = n_ref[h]

        d_log = jnp.where(tril, b_col - b_row + i_row, -jnp.inf)
        g_inter = b_col + m_prev
        m_row = jnp.maximum(g_inter, jnp.max(d_log, axis=-1, keepdims=True))
        w_intra = jnp.exp(d_log - m_row)
        w_inter = jnp.exp(g_inter - m_row)
        qk = lax.dot_general(q, k, (((1,), (1,)), ((), ())), preferred_element_type=F32)
        scores = qk * w_intra
        num = (w_inter * jnp.dot(q, c_prev.astype(BF16), preferred_element_type=F32)
               + jnp.dot(scores.astype(BF16), v, preferred_element_type=F32))
        den = (w_inter * jnp.sum(q.astype(F32) * n_prev, axis=-1, keepdims=True)
               + jnp.sum(scores, axis=-1, keepdims=True))
        hh = num / jnp.maximum(jnp.abs(den), jnp.exp(-m_row))

        b_last = b_col[L - 1:L, :]
        a_log = b_last - b_col + i_col
        m_new = jnp.maximum(b_last + m_prev, jnp.max(a_log, axis=0, keepdims=True))
        decay = jnp.exp(b_last + m_prev - m_new)
        w_upd = jnp.exp(a_log - m_new)
        kw = k.astype(F32) * w_upd
        c_ref[h] = decay * c_prev + jnp.dot(kw.T.astype(BF16), v, preferred_element_type=F32)
        n_ref[h] = decay * n_prev + jnp.sum(kw, axis=0, keepdims=True)
        m_ref[h] = jnp.broadcast_to(m_new, m_ref.shape[1:])

        ms = jnp.mean(hh * hh, axis=-1, keepdims=True)
        y = (hh * lax.rsqrt(ms + EPS) * gout_ref[:, vs]
             * jax.nn.sigmoid(mo_ref[0, :, vs].astype(F32)))
        o_ref[0, :, vs] = y.astype(o_ref.dtype)


def _mlstm(proj3d, gate_cols, gate_rows, gout, *, chunk):
    bsz, seq, _ = proj3d.shape
    return pl.pallas_call(
        _mlstm_kernel,
        grid=(bsz, seq // chunk),
        in_specs=[
            pl.BlockSpec((1, chunk, ML_QK_W), lambda b, c: (b, c, COL_MQ // ML_QK_W)),
            pl.BlockSpec((1, chunk, ML_QK_W), lambda b, c: (b, c, COL_MK // ML_QK_W)),
            pl.BlockSpec((1, chunk, ML_V_W), lambda b, c: (b, c, COL_MV // ML_V_W)),
            pl.BlockSpec((1, chunk, ML_V_W), lambda b, c: (b, c, COL_MO // ML_V_W)),
            pl.BlockSpec((1, chunk, LANES), lambda b, c: (b, c, 0)),
            pl.BlockSpec((1, 16, chunk), lambda b, c: (b, 0, c)),
            pl.BlockSpec((1, ML_V_W), lambda b, c: (0, 0)),
        ],
        out_specs=pl.BlockSpec((1, chunk, ML_V_W), lambda b, c: (b, c, 0)),
        out_shape=jax.ShapeDtypeStruct((bsz, seq, ML_V_W), BF16),
        scratch_shapes=[
            pltpu.VMEM((ML_HEADS, ML_QK_DIM, ML_V_DIM), F32),
            pltpu.VMEM((ML_HEADS, 1, ML_QK_DIM), F32),
            pltpu.VMEM((ML_HEADS, 8, LANES), F32),
        ],
        compiler_params=_cparams(2),
        name="mlstm",
    )(proj3d, proj3d, proj3d, proj3d, gate_cols, gate_rows, gout)


def _merge_kernel(ya_ref, yb_ref, ga_ref, gb_ref, x_ref, wa_ref, wb_ref, wo_ref, g2_ref,
                  wr_ref, br_ref, h_ref, tp_ref, route_ref, cnt_ref, carry_ref):
    i = pl.program_id(0)
    tm = x_ref.shape[0]

    @pl.when(i == 0)
    def _():
        carry_ref[...] = jnp.zeros_like(carry_ref)

    a = jnp.dot(ya_ref[...], wa_ref[...], preferred_element_type=F32)
    b = jnp.dot(yb_ref[...], wb_ref[...], preferred_element_type=F32)
    merged = (jax.nn.sigmoid(ga_ref[...].astype(F32)) * a
              + jax.nn.sigmoid(gb_ref[...].astype(F32)) * b)
    h = x_ref[...] + jnp.dot(merged.astype(BF16), wo_ref[...], preferred_element_type=F32)
    h_ref[...] = h
    t = _rms_norm_rows(h, g2_ref[...])
    tp_ref[...] = _pack_rows(t)

    t_hi = t.astype(BF16)
    t_lo = (t - t_hi.astype(F32)).astype(BF16)
    w_r = wr_ref[...]
    w_hi = w_r.astype(BF16)
    w_lo = (w_r - w_hi.astype(F32)).astype(BF16)
    logits = (jnp.dot(t_hi, w_hi, preferred_element_type=F32)
              + jnp.dot(t_lo, w_hi, preferred_element_type=F32)
              + jnp.dot(t_hi, w_lo, preferred_element_type=F32)
              + br_ref[...])
    lane = lax.broadcasted_iota(jnp.int32, (tm, LANES), 1).astype(F32)
    lg = logits
    sels, vals, idxs = [], [], []
    for _ in range(TOP_K):
        mx = jnp.max(lg, axis=-1, keepdims=True)
        ik = jnp.min(jnp.where(lg == mx, lane, float(LANES)), axis=-1, keepdims=True)
        sel = lane == ik
        sels.append(sel)
        vals.append(mx)
        idxs.append(ik)
        lg = jnp.where(sel, -jnp.inf, lg)
    exps = [jnp.exp(v - vals[0]) for v in vals]
    den = exps[0] + exps[1] + exps[2] + exps[3]
    mask = jnp.zeros((tm, LANES), F32)
    for sel in sels:
        mask = mask + jnp.where(sel, 1.0, 0.0)
    row = lax.broadcasted_iota(jnp.int32, (tm, tm), 0)
    col = lax.broadcasted_iota(jnp.int32, (tm, tm), 1)
    strict = jnp.where(col < row, 1.0, 0.0).astype(BF16)
    ranks = jnp.dot(strict, mask.astype(BF16), preferred_element_type=F32) + carry_ref[...]
    slab = jnp.zeros((tm, LANES), F32)
    for kk in range(TOP_K):
        rank_k = jnp.sum(jnp.where(sels[kk], ranks, 0.0), axis=-1, keepdims=True)
        slab = jnp.where(lane == float(kk), idxs[kk], slab)
        slab = jnp.where(lane == float(TOP_K + kk), rank_k, slab)
        slab = jnp.where(lane == float(2 * TOP_K + kk), exps[kk] / den, slab)
    route_ref[...] = slab
    new_carry = carry_ref[...] + jnp.sum(mask, axis=0, keepdims=True)
    carry_ref[...] = new_carry
    cnt_ref[...] = new_carry


def _merge(y_a, y_b, proj, x2d, w_a, w_b, w_o, g2, w_r, b_r, *, tm):
    t_rows = x2d.shape[0]
    const = lambda shape: pl.BlockSpec(shape, lambda i: (0, 0), pipeline_mode=pl.Buffered(1))
    return pl.pallas_call(
        _merge_kernel,
        grid=(t_rows // tm,),
        in_specs=[
            pl.BlockSpec((tm, FOX_W), lambda i: (i, 0)),
            pl.BlockSpec((tm, ML_V_W), lambda i: (i, 0)),
            pl.BlockSpec((tm, D_MODEL), lambda i: (i, COL_GA // D_MODEL)),
            pl.BlockSpec((tm, D_MODEL), lambda i: (i, COL_GB // D_MODEL)),
            pl.BlockSpec((tm, D_MODEL), lambda i: (i, 0)),
            const((FOX_W, D_MODEL)),
            const((ML_V_W, D_MODEL)),
            const((D_MODEL, D_MODEL)),
            const((1, D_MODEL)),
            const((D_MODEL, LANES)),
            const((1, LANES)),
        ],
        out_specs=[
            pl.BlockSpec((tm, D_MODEL), lambda i: (i, 0)),
            pl.BlockSpec((tm, HALF), lambda i: (i, 0)),
            pl.BlockSpec((tm, LANES), lambda i: (i, 0)),
            pl.BlockSpec((1, LANES), lambda i: (0, 0)),
        ],
        out_shape=[
            jax.ShapeDtypeStruct((t_rows, D_MODEL), F32),
            jax.ShapeDtypeStruct((t_rows, HALF), jnp.uint32),
            jax.ShapeDtypeStruct((t_rows, LANES), F32),
            jax.ShapeDtypeStruct((1, LANES), F32),
        ],
        scratch_shapes=[pltpu.VMEM((1, LANES), F32)],
        compiler_params=_cparams(1),
        name="merge_router",
    )(y_a, y_b, proj, proj, x2d, w_a, w_b, w_o, g2, w_r, b_r)


def _dispatch_kernel(pos_ref, zt_ref, tp_ref, xs_ref, stage_ref, zero_ref, sem, zsem, *, tr, tile):
    i = pl.program_id(0)
    n = pl.num_programs(0)

    @pl.when(i == 0)
    def _():
        zero_ref[...] = jnp.zeros_like(zero_ref)

        def zero_copy(z):
            start = pl.multiple_of(jnp.maximum(zt_ref[z], 0) * tile, tile)
            return pltpu.make_async_copy(zero_ref, xs_ref.at[pl.ds(start, tile), :], zsem)

        def start_body(z, carry):
            @pl.when(zt_ref[z] >= 0)
            def _():
                zero_copy(z).start()
            return carry
        lax.fori_loop(0, zt_ref.shape[0], start_body, 0)

        def wait_body(z, carry):
            @pl.when(zt_ref[z] >= 0)
            def _():
                zero_copy(z).wait()
            return carry
        lax.fori_loop(0, zt_ref.shape[0], wait_body, 0)

    def row_copy(s, r, p):
        return pltpu.make_async_copy(stage_ref.at[s, pl.ds(r, 1), :], xs_ref.at[pl.ds(p, 1), :], sem.at[s])

    slot = i % 2
    stage_ref[slot] = tp_ref[...]
    base = i * (tr * TOP_K)

    def issue(r, carry):
        for kk in range(TOP_K):
            row_copy(slot, r, pos_ref[base + r * TOP_K + kk]).start()
        return carry
    lax.fori_loop(0, tr, issue, 0, unroll=4)

    def drain(s):
        def body(r, carry):
            for kk in range(TOP_K):
                row_copy(s, r, 0).wait()
            return carry
        lax.fori_loop(0, tr, body, 0)

    @pl.when(i > 0)
    def _():
        drain(1 - slot)

    @pl.when(i == n - 1)
    def _():
        drain(slot)


def _dispatch(pos_flat, zero_tiles, tpk, *, n_rows, tr, tile):
    t_rows = tpk.shape[0]
    return pl.pallas_call(
        functools.partial(_dispatch_kernel, tr=tr, tile=tile),
        grid_spec=pltpu.PrefetchScalarGridSpec(
            num_scalar_prefetch=2,
            grid=(t_rows // tr,),
            in_specs=[pl.BlockSpec((tr, HALF), lambda i, pos, zt: (i, 0))],
            out_specs=pl.BlockSpec(memory_space=pl.ANY),
            scratch_shapes=[
                pltpu.VMEM((2, tr, HALF), jnp.uint32),
                pltpu.VMEM((tile, HALF), jnp.uint32),
                pltpu.SemaphoreType.DMA((2,)),
                pltpu.SemaphoreType.DMA(()),
            ],
        ),
        out_shape=jax.ShapeDtypeStruct((n_rows, HALF), jnp.uint32),
        compiler_params=_cparams(1),
        name="dispatch",
    )(pos_flat, zero_tiles, tpk)


def _experts_kernel(te_ref, nv_ref, nu_ref, xs_ref, wg_ref, bg_ref, wu_ref, bu_ref, wd_ref, bd_ref,
                    o_ref, xb_ref, acc_ref, *, sub):
    del te_ref, nu_ref
    i = pl.program_id(0)
    j = pl.program_id(1)
    n_ff = pl.num_programs(1)
    tm = xs_ref.shape[0]
    valid = nv_ref[i]

    @pl.when(j == 0)
    def _():
        lo, hi = _unpack_rows(xs_ref[...])
        xb_ref[:, :HALF] = lo.astype(BF16)
        xb_ref[:, HALF:] = hi.astype(BF16)
        acc_ref[...] = jnp.broadcast_to(bd_ref[0], acc_ref.shape)

    def ffn_chunk(n_rows):
        xb = xb_ref[0:n_rows, :]
        g = jnp.dot(xb, wg_ref[0].astype(BF16), preferred_element_type=F32) + bg_ref[0]
        u = jnp.dot(xb, wu_ref[0].astype(BF16), preferred_element_type=F32) + bu_ref[0]
        g = jnp.minimum(g, SWIGLU_LIMIT)
        u = jnp.clip(u, -SWIGLU_LIMIT, SWIGLU_LIMIT)
        act = (u + 1.0) * g * jax.nn.sigmoid(SWIGLU_ALPHA * g)
        acc_ref[0:n_rows, :] += jnp.dot(act.astype(BF16), wd_ref[0].astype(BF16),
                                        preferred_element_type=F32)

    n_sub = tm // sub
    for s in range(n_sub):
        lo_rows, hi_rows = s * sub, (s + 1) * sub
        upper = (valid <= hi_rows) if s + 1 < n_sub else True

        @pl.when((valid > lo_rows) & upper)
        def _():
            ffn_chunk(hi_rows)

    @pl.when(j == n_ff - 1)
    def _():
        o_ref[...] = _pack_rows(acc_ref[...])


def _experts(tile_expert, tile_valid, n_used, xs, w_gate, b_gate, w_up, b_up, w_down, b_down,
             *, tm, tf, sub):
    n_tiles = xs.shape[0] // tm
    n_ff = D_FF // tf

    def tile_i(i, nu):
        return jnp.maximum(jnp.minimum(i, nu[0] - 1), 0)

    def ff_j(i, j, nu):
        return jnp.where(i < nu[0], j, n_ff - 1)

    return pl.pallas_call(
        functools.partial(_experts_kernel, sub=sub),
        grid_spec=pltpu.PrefetchScalarGridSpec(
            num_scalar_prefetch=3,
            grid=(n_tiles, n_ff),
            in_specs=[
                pl.BlockSpec((tm, HALF), lambda i, j, te, nv, nu: (tile_i(i, nu), 0)),
                pl.BlockSpec((1, D_MODEL, tf), lambda i, j, te, nv, nu: (te[i], 0, ff_j(i, j, nu))),
                pl.BlockSpec((1, 1, tf), lambda i, j, te, nv, nu: (te[i], 0, ff_j(i, j, nu))),
                pl.BlockSpec((1, D_MODEL, tf), lambda i, j, te, nv, nu: (te[i], 0, ff_j(i, j, nu))),
                pl.BlockSpec((1, 1, tf), lambda i, j, te, nv, nu: (te[i], 0, ff_j(i, j, nu))),
                pl.BlockSpec((1, tf, D_MODEL), lambda i, j, te, nv, nu: (te[i], ff_j(i, j, nu), 0)),
                pl.BlockSpec((1, 1, D_MODEL), lambda i, j, te, nv, nu: (te[i], 0, 0)),
            ],
            out_specs=pl.BlockSpec((tm, HALF), lambda i, j, te, nv, nu: (i, 0)),
            scratch_shapes=[
                pltpu.VMEM((tm, D_MODEL), BF16),
                pltpu.VMEM((tm, D_MODEL), F32),
            ],
        ),
        out_shape=jax.ShapeDtypeStruct(xs.shape, jnp.uint32),
        compiler_params=_cparams(2),
        name="experts",
    )(tile_expert, tile_valid, n_used, xs, w_gate, b_gate, w_up, b_up, w_down, b_down)


def _combine_kernel(pos_ref, h_ref, w_ref, ys_ref, o_ref, buf_ref, sem):
    i = pl.program_id(0)
    n = pl.num_programs(0)
    tc = h_ref.shape[0]

    def row_copy(p, slot, kk, r):
        return pltpu.make_async_copy(ys_ref.at[pl.ds(p, 1), :],
                                     buf_ref.at[slot, kk, pl.ds(r, 1), :], sem.at[slot])

    def issue(step, slot):
        base = step * (tc * TOP_K)

        def body(r, carry):
            for kk in range(TOP_K):
                row_copy(pos_ref[base + r * TOP_K + kk], slot, kk, r).start()
            return carry
        lax.fori_loop(0, tc, body, 0, unroll=4)

    @pl.when(i == 0)
    def _():
        issue(0, 0)

    @pl.when(i + 1 < n)
    def _():
        issue(i + 1, (i + 1) % 2)

    slot = i % 2

    def drain(r, carry):
        for kk in range(TOP_K):
            row_copy(0, slot, kk, r).wait()
        return carry
    lax.fori_loop(0, tc, drain, 0)

    w = w_ref[...]
    acc_lo = h_ref[:, :HALF]
    acc_hi = h_ref[:, HALF:]
    for kk in range(TOP_K):
        lo, hi = _unpack_rows(buf_ref[slot, kk])
        acc_lo = acc_lo + w[:, kk:kk + 1] * lo
        acc_hi = acc_hi + w[:, kk:kk + 1] * hi
    o_ref[:, :HALF] = acc_lo
    o_ref[:, HALF:] = acc_hi


def _combine(pos_flat, h2d, w_top, ys, *, tc):
    t_rows = h2d.shape[0]
    return pl.pallas_call(
        _combine_kernel,
        grid_spec=pltpu.PrefetchScalarGridSpec(
            num_scalar_prefetch=1,
            grid=(t_rows // tc,),
            in_specs=[
                pl.BlockSpec((tc, D_MODEL), lambda i, pos: (i, 0)),
                pl.BlockSpec((tc, TOP_K), lambda i, pos: (i, 0)),
                pl.BlockSpec(memory_space=pl.ANY),
            ],
            out_specs=pl.BlockSpec((tc, D_MODEL), lambda i, pos: (i, 0)),
            scratch_shapes=[
                pltpu.VMEM((2, TOP_K, tc, HALF), jnp.uint32),
                pltpu.SemaphoreType.DMA((2,)),
            ],
        ),
        out_shape=jax.ShapeDtypeStruct(h2d.shape, F32),
        compiler_params=_cparams(1),
        name="combine",
    )(pos_flat, h2d, w_top, ys)


def _pick(n, pref):
    t = min(n, pref)
    assert n % t == 0, (n, t)
    return t


def kernel(x, norm1_g, w_in, fox_f_bias, q_norm_g, k_norm_g, ml_i_bias, ml_f_bias, ml_out_norm_g,
           w_branch_a, w_branch_b, w_out, norm2_g, w_router, b_router,
           w_gate, b_gate, w_up, b_up, w_down, b_down):
    bsz, seq, d_model = x.shape
    assert d_model == D_MODEL and norm1_g.shape[0] == 1, "single-layer block of width 2048"
    t_rows = bsz * seq
    x2d = x.reshape(t_rows, D_MODEL).astype(F32)

    offs = [0]
    for wdt in IN_WIDTHS:
        offs.append(offs[-1] + wdt)
    seg = lambda n: w_in[0][:, offs[n]:offs[n + 1]]
    w_main = jnp.concatenate([w_in[0][:, offs[0]:offs[3]].astype(BF16), w_in[0][:, offs[4]:offs[7]].astype(BF16),
                              w_in[0][:, offs[9]:offs[12]].astype(BF16)], axis=1)
    w_gl = jnp.concatenate([seg(3), seg(7), seg(8),
                            jnp.zeros((D_MODEL, LANES - FOX_HEADS - 2 * ML_HEADS), F32)], axis=1).astype(BF16)
    colscale = jnp.ones((PROJ_W,), F32)
    colscale = colscale.at[COL_FQ:COL_FQ + FOX_W].set(
        jnp.tile(q_norm_g[0].astype(F32), FOX_HEADS) * (FOX_HEAD_DIM ** -0.5 * LOG2_E))
    colscale = colscale.at[COL_FK:COL_FK + FOX_W].set(jnp.tile(k_norm_g[0].astype(F32), FOX_HEADS))
    colscale = colscale.at[COL_MK:COL_MK + ML_QK_W].set(ML_QK_DIM ** -0.5)
    colscale = colscale.reshape(1, PROJ_W)
    gate_bias = jnp.concatenate([fox_f_bias[0], ml_i_bias[0], ml_f_bias[0],
                                 jnp.zeros((LANES - FOX_HEADS - 2 * ML_HEADS,), F32)]).astype(F32).reshape(1, LANES)
    g2 = norm2_g.astype(F32).reshape(1, D_MODEL)

    proj, gates = _in_proj(x2d, norm1_g.astype(F32).reshape(1, D_MODEL), w_main, w_gl, colscale,
                           tm=_pick(t_rows, 1024), tn=512)
    chunk = _pick(seq, 256)
    gp = _gate_prep(gates.reshape(bsz, seq, LANES), gate_bias, ts=_pick(seq, 512), chunk=chunk)
    gp_rows = jnp.transpose(gp[:, :, :16], (0, 2, 1))
    fox_group = 2
    neg_cum_f = (-LOG2_E * gp_rows[:, GCOL_FF:GCOL_FF + FOX_HEADS, :]).reshape(
        bsz * FOX_HEADS // fox_group, fox_group, seq)
    proj3d = proj.reshape(bsz, seq, PROJ_W)
    qk_bound = (1.02 * FOX_HEAD_DIM * (FOX_HEAD_DIM ** -0.5 * LOG2_E)
                * jnp.max(jnp.abs(q_norm_g[0].astype(F32))) * jnp.max(jnp.abs(k_norm_g[0].astype(F32))))
    y_a = _fox_attention(proj3d, neg_cum_f, qk_bound, tq=_pick(seq, 512), heads=fox_group)
    y_b = _mlstm(proj3d, gp, gp_rows, ml_out_norm_g.astype(F32).reshape(1, ML_V_W), chunk=chunk)

    w_r = jnp.concatenate([w_router[0].astype(F32), jnp.zeros((D_MODEL, LANES - N_EXPERTS), F32)], axis=1)
    b_r = jnp.concatenate([b_router[0].astype(F32), jnp.full((LANES - N_EXPERTS,), -jnp.inf, F32)]).reshape(1, LANES)
    h2d, tpk, route, counts = _merge(
        y_a.reshape(t_rows, FOX_W), y_b.reshape(t_rows, ML_V_W), proj, x2d,
        w_branch_a[0].astype(BF16), w_branch_b[0].astype(BF16), w_out[0].astype(BF16),
        g2, w_r, b_r, tm=_pick(t_rows, 512))

    tm_e = _pick(t_rows * TOP_K, 1024)
    n_tiles = (t_rows * TOP_K) // tm_e + N_EXPERTS
    top_idx = route[:, 0:TOP_K].astype(jnp.int32)
    top_rank = route[:, TOP_K:2 * TOP_K].astype(jnp.int32)
    top_w = route[:, 2 * TOP_K:3 * TOP_K]
    cnt = counts[0, :N_EXPERTS].astype(jnp.int32)
    tiles_per_e = (cnt + tm_e - 1) // tm_e
    tile_end = jnp.cumsum(tiles_per_e)
    tile_begin = tile_end - tiles_per_e
    pos_flat = ((tile_begin * tm_e)[top_idx] + top_rank).reshape(-1)
    n_used = tile_end[-1:].astype(jnp.int32)
    tile_ids = jnp.arange(n_tiles, dtype=jnp.int32)
    used_ids = jnp.minimum(tile_ids, n_used[0] - 1)
    tile_expert = jnp.sum((used_ids[:, None] >= tile_end[None, :]).astype(jnp.int32), axis=1)
    tile_valid = jnp.clip(cnt[tile_expert] - (used_ids - tile_begin[tile_expert]) * tm_e, 0, tm_e)
    tile_valid = jnp.where(tile_ids < n_used[0], tile_valid, 0).astype(jnp.int32)

    last_tile = jnp.where(tiles_per_e > 0, tile_end - 1, -1)
    tail_ids = tile_ids[n_tiles - N_EXPERTS:]
    zero_tiles = jnp.concatenate([last_tile, jnp.where(tail_ids >= n_used[0], tail_ids, -1)]).astype(jnp.int32)
    xs = _dispatch(pos_flat, zero_tiles, tpk, n_rows=n_tiles * tm_e, tr=_pick(t_rows, 256), tile=tm_e)
    ys = _experts(tile_expert, tile_valid, n_used, xs,
                  w_gate[0], b_gate[0].reshape(N_EXPERTS, 1, D_FF),
                  w_up[0], b_up[0].reshape(N_EXPERTS, 1, D_FF),
                  w_down[0], b_down[0].reshape(N_EXPERTS, 1, D_MODEL),
                  tm=tm_e, tf=256, sub=_pick(tm_e, 256))
    out = _combine(pos_flat, h2d, top_w, ys, tc=_pick(t_rows, 128))
    return out.reshape(bsz, seq, D_MODEL).astype(x.dtype)
```

```python
import functools

import jax
import jax.numpy as jnp
from jax import lax
from jax.experimental import pallas as pl
from jax.experimental.pallas import tpu as pltpu

F32 = jnp.float32
BF16 = jnp.bfloat16

D_MODEL = 2048
FOX_HEADS = 8
FOX_HEAD_DIM = 128
FOX_W = FOX_HEADS * FOX_HEAD_DIM
ML_HEADS = 4
ML_QK_DIM = 128
ML_V_DIM = 256
ML_QK_W = ML_HEADS * ML_QK_DIM
ML_V_W = ML_HEADS * ML_V_DIM
IGATE_CAP = 15.0
N_EXPERTS = 32
TOP_K = 4
D_FF = 2048
SWIGLU_LIMIT = 7.0
SWIGLU_ALPHA = 1.702
EPS = 1e-5

IN_WIDTHS = (FOX_W, FOX_W, FOX_W, FOX_HEADS,
             ML_QK_W, ML_QK_W, ML_V_W, ML_HEADS, ML_HEADS, ML_V_W,
             D_MODEL, D_MODEL)

LANES = 128
VMEM_LIMIT = 56 * 1024 * 1024

COL_FQ, COL_FK, COL_FV = 0, FOX_W, 2 * FOX_W
COL_MQ = 3 * FOX_W
COL_MK = COL_MQ + ML_QK_W
COL_MV = COL_MK + ML_QK_W
COL_MO = COL_MV + ML_V_W
COL_GA = COL_MO + ML_V_W
COL_GB = COL_GA + D_MODEL
PROJ_W = COL_GB + D_MODEL
GCOL_FF, GCOL_MI, GCOL_MF = 0, FOX_HEADS, FOX_HEADS + ML_HEADS

HALF = D_MODEL // 2
LOG2_E = 1.4426950408889634


def _cparams(n_axes, vmem=VMEM_LIMIT):
    return pltpu.CompilerParams(dimension_semantics=("arbitrary",) * n_axes,
                                vmem_limit_bytes=vmem)


def _log_sigmoid(x):
    return jnp.minimum(x, 0.0) - jnp.log1p(jnp.exp(-jnp.abs(x)))


def _pack_rows(x):
    lo = lax.bitcast_convert_type(x[:, :HALF].astype(BF16).astype(F32), jnp.uint32)
    hi = lax.bitcast_convert_type(x[:, HALF:].astype(BF16).astype(F32), jnp.uint32)
    return hi | (lo >> 16)


def _unpack_rows(p):
    lo = lax.bitcast_convert_type(p << 16, F32)
    hi = lax.bitcast_convert_type(p & jnp.uint32(0xFFFF0000), F32)
    return lo, hi


def _rms_norm_rows(h, g):
    ms = jnp.mean(h * h, axis=-1, keepdims=True)
    return h * lax.rsqrt(ms + EPS) * g


def _in_proj_kernel(x_ref, g1_ref, w_ref, wg_ref, cs_ref, o_ref, gate_ref, xn_ref,
                    *, n_norm_blocks, row_chunk):
    j = pl.program_id(1)
    tm = x_ref.shape[0]
    tn = w_ref.shape[1]

    @pl.when(j == 0)
    def _():
        def body(c, carry):
            r0 = pl.multiple_of(c * row_chunk, row_chunk)
            x = x_ref[pl.ds(r0, row_chunk), :]
            ms = jnp.mean(x * x, axis=-1, keepdims=True)
            xn_ref[pl.ds(r0, row_chunk), :] = (x * lax.rsqrt(ms + EPS) * g1_ref[...]).astype(BF16)
            return carry
        lax.fori_loop(0, tm // row_chunk, body, 0)
        gate_ref[...] = jnp.dot(xn_ref[...], wg_ref[...], preferred_element_type=F32)

    @pl.when(j < n_norm_blocks)
    def _():
        half = tn // 2
        for c0 in (0, half):
            acc = jnp.dot(xn_ref[...], w_ref[:, c0:c0 + half], preferred_element_type=F32)
            for s in range(half // LANES):
                a = acc[:, s * LANES:(s + 1) * LANES]
                cols = slice(c0 + s * LANES, c0 + (s + 1) * LANES)
                ms = jnp.mean(a * a, axis=-1, keepdims=True)
                o_ref[:, cols] = (a * lax.rsqrt(ms + EPS) * cs_ref[:, cols]).astype(o_ref.dtype)

    @pl.when(j >= n_norm_blocks)
    def _():
        half = tn // 2
        for c0 in (0, half):
            acc = jnp.dot(xn_ref[...], w_ref[:, c0:c0 + half], preferred_element_type=F32)
            o_ref[:, c0:c0 + half] = (acc * cs_ref[:, c0:c0 + half]).astype(o_ref.dtype)


def _in_proj(x2d, g1, w_main, w_gate, colscale, *, tm, tn):
    t_rows = x2d.shape[0]
    grid = (t_rows // tm, PROJ_W // tn)
    kern = functools.partial(_in_proj_kernel, n_norm_blocks=(2 * FOX_W) // tn, row_chunk=128)
    return pl.pallas_call(
        kern,
        grid=grid,
        in_specs=[
            pl.BlockSpec((tm, D_MODEL), lambda i, j: (i, 0)),
            pl.BlockSpec((1, D_MODEL), lambda i, j: (0, 0)),
            pl.BlockSpec((D_MODEL, tn), lambda i, j: (0, j)),
            pl.BlockSpec((D_MODEL, LANES), lambda i, j: (0, 0)),
            pl.BlockSpec((1, tn), lambda i, j: (0, j)),
        ],
        out_specs=[
            pl.BlockSpec((tm, tn), lambda i, j: (i, j)),
            pl.BlockSpec((tm, LANES), lambda i, j: (i, 0)),
        ],
        out_shape=[
            jax.ShapeDtypeStruct((t_rows, PROJ_W), BF16),
            jax.ShapeDtypeStruct((t_rows, LANES), F32),
        ],
        scratch_shapes=[pltpu.VMEM((tm, D_MODEL), BF16)],
        compiler_params=_cparams(2),
        name="in_proj",
    )(x2d, g1, w_main, w_gate, colscale)


def _gate_prep_kernel(g_ref, b_ref, o_ref, carry_ref, *, chunk):
    s = pl.program_id(1)
    ts = g_ref.shape[1]

    @pl.when(s == 0)
    def _():
        carry_ref[...] = jnp.zeros_like(carry_ref)

    z = g_ref[0] + b_ref[...]
    log_f = _log_sigmoid(z)
    i_pre = IGATE_CAP * jnp.tanh(z / IGATE_CAP)
    row = lax.broadcasted_iota(jnp.int32, (ts, ts), 0)
    col = lax.broadcasted_iota(jnp.int32, (ts, ts), 1)
    tril = (col <= row)
    same_chunk = (row // chunk) == (col // chunk)
    tril_f = jnp.where(tril, 1.0, 0.0).astype(F32)
    tril_c = jnp.where(tril & same_chunk, 1.0, 0.0).astype(F32)
    run_sum = jnp.dot(tril_f, log_f, preferred_element_type=F32,
                      precision=lax.Precision.HIGHEST) + carry_ref[...]
    chunk_sum = jnp.dot(tril_c, log_f, preferred_element_type=F32,
                        precision=lax.Precision.HIGHEST)
    carry_ref[...] = run_sum[ts - 1:ts, :]
    lane = lax.broadcasted_iota(jnp.int32, (ts, LANES), 1)
    o_ref[0] = jnp.where(lane < GCOL_MI, run_sum, jnp.where(lane < GCOL_MF, i_pre, chunk_sum))


def _gate_prep(gates3d, bias, *, ts, chunk):
    bsz, seq, _ = gates3d.shape
    return pl.pallas_call(
        functools.partial(_gate_prep_kernel, chunk=chunk),
        grid=(bsz, seq // ts),
        in_specs=[
            pl.BlockSpec((1, ts, LANES), lambda b, s: (b, s, 0)),
            pl.BlockSpec((1, LANES), lambda b, s: (0, 0)),
        ],
        out_specs=pl.BlockSpec((1, ts, LANES), lambda b, s: (b, s, 0)),
        out_shape=jax.ShapeDtypeStruct((bsz, seq, LANES), F32),
        scratch_shapes=[pltpu.VMEM((1, LANES), F32)],
        compiler_params=_cparams(2),
        name="gate_prep",
    )(gates3d, bias)


def _fox_kernel(lo_ref, q_ref, k_ref, v_ref, ncf_ref, o_ref, *, tq):
    qi = pl.program_id(2)
    first = lo_ref[(pl.program_id(0) * pl.num_programs(1) + pl.program_id(1)) * pl.num_programs(2) + qi]
    hd = FOX_HEAD_DIM
    heads = q_ref.shape[-1] // hd

    def block(kb, carry, masked):
        off = pl.multiple_of(kb * tq, tq)
        new = []
        for hh in range(heads):
            cols = slice(hh * hd, (hh + 1) * hd)
            m, l, acc = carry[hh]
            q = q_ref[0, :, cols]
            k = k_ref[0, pl.ds(off, tq), cols]
            v = v_ref[0, pl.ds(off, tq), cols]
            s = lax.dot_general(q, k, (((1,), (1,)), ((), ())), preferred_element_type=F32)
            s = s + ncf_ref[0, hh:hh + 1, pl.ds(off, tq)]
            if masked:
                row = lax.broadcasted_iota(jnp.int32, (tq, tq), 0)
                col = lax.broadcasted_iota(jnp.int32, (tq, tq), 1)
                s = jnp.where(row >= col, s, -jnp.inf)
            m_new = jnp.maximum(m, jnp.max(s, axis=-1, keepdims=True))
            alpha = jnp.exp2(m - m_new)
            p = jnp.exp2(s - m_new)
            l = alpha * l + jnp.sum(p, axis=-1, keepdims=True)
            acc = alpha * acc + jnp.dot(p.astype(BF16), v, preferred_element_type=F32)
            new.append((m_new, l, acc))
        return tuple(new)

    init = tuple((jnp.full((tq, 1), -jnp.inf, F32), jnp.zeros((tq, 1), F32), jnp.zeros((tq, hd), F32))
                 for _ in range(heads))
    carry = lax.fori_loop(first, qi, lambda kb, c: block(kb, c, False), init)
    final = block(qi, carry, True)
    for hh in range(heads):
        _, l, acc = final[hh]
        o_ref[0, :, hh * hd:(hh + 1) * hd] = (acc / l).astype(o_ref.dtype)


FOX_PRUNE_MARGIN = 150.0


def _fox_first_block(neg_cum_f, qk_bound, *, tq, heads):
    n_seq, _, seq = neg_cum_f.shape[0], neg_cum_f.shape[1], neg_cum_f.shape[2]
    nq = seq // tq
    blk = neg_cum_f.reshape(n_seq, heads, nq, tq)
    blk_max = lax.cummax(jnp.max(blk, axis=-1), axis=2)
    row_min = jnp.min(blk, axis=-1)
    thresh = row_min - (2.0 * qk_bound + FOX_PRUNE_MARGIN)
    skippable = blk_max[:, :, None, :] < thresh[:, :, :, None]
    first = jnp.sum(skippable.astype(jnp.int32), axis=-1)
    first = jnp.min(first, axis=1)
    first = jnp.minimum(first, jnp.arange(nq, dtype=jnp.int32)[None, :])
    return first.reshape(-1).astype(jnp.int32)


def _fox_attention(proj3d, neg_cum_f, qk_bound, *, tq, heads):
    bsz, seq, _ = proj3d.shape
    hw = heads * FOX_HEAD_DIM
    groups = FOX_HEADS // heads
    first_block = _fox_first_block(neg_cum_f, qk_bound, tq=tq, heads=heads)
    return pl.pallas_call(
        functools.partial(_fox_kernel, tq=tq),
        grid_spec=pltpu.PrefetchScalarGridSpec(
            num_scalar_prefetch=1,
            grid=(bsz, groups, seq // tq),
            in_specs=[
                pl.BlockSpec((1, tq, hw), lambda b, g, i, lo: (b, i, COL_FQ // hw + g)),
                pl.BlockSpec((1, seq, hw), lambda b, g, i, lo: (b, 0, COL_FK // hw + g)),
                pl.BlockSpec((1, seq, hw), lambda b, g, i, lo: (b, 0, COL_FV // hw + g)),
                pl.BlockSpec((1, heads, seq), lambda b, g, i, lo: (b * groups + g, 0, 0)),
            ],
            out_specs=pl.BlockSpec((1, tq, hw), lambda b, g, i, lo: (b, i, g)),
        ),
        out_shape=jax.ShapeDtypeStruct((bsz, seq, FOX_W), BF16),
        compiler_params=_cparams(3),
        name="fox_attn",
    )(first_block, proj3d, proj3d, proj3d, neg_cum_f)


def _mlstm_kernel(q_ref, k_ref, v_ref, mo_ref, gc_ref, gr_ref, gout_ref, o_ref,
                  c_ref, n_ref, m_ref):
    c_idx = pl.program_id(1)
    L = q_ref.shape[1]

    @pl.when(c_idx == 0)
    def _():
        c_ref[...] = jnp.zeros_like(c_ref)
        n_ref[...] = jnp.zeros_like(n_ref)
        m_ref[...] = jnp.zeros_like(m_ref)

    row = lax.broadcasted_iota(jnp.int32, (L, L), 0)
    col = lax.broadcasted_iota(jnp.int32, (L, L), 1)
    tril = col <= row

    for h in range(ML_HEADS):
        qs = slice(h * ML_QK_DIM, (h + 1) * ML_QK_DIM)
        vs = slice(h * ML_V_DIM, (h + 1) * ML_V_DIM)
        q = q_ref[0, :, qs]
        k = k_ref[0, :, qs]
        v = v_ref[0, :, vs]
        b_col = gc_ref[0, :, GCOL_MF + h:GCOL_MF + h + 1]
        i_col = gc_ref[0, :, GCOL_MI + h:GCOL_MI + h + 1]
        b_row = gr_ref[0, GCOL_MF + h:GCOL_MF + h + 1, :]
        i_row = gr_ref[0, GCOL_MI + h:GCOL_MI + h + 1, :]
        m_prev = m_ref[h, 0:1, 0:1]
        c_prev = c_ref[h]
        n_prev = n_ref[h]

        d_log = jnp.where(tril, b_col - b_row + i_row, -jnp.inf)
        g_inter = b_col + m_prev
        m_row = jnp.maximum(g_inter, jnp.max(d_log, axis=-1, keepdims=True))
        w_intra = jnp.exp(d_log - m_row)
        w_inter = jnp.exp(g_inter - m_row)
        qk = lax.dot_general(q, k, (((1,), (1,)), ((), ())), preferred_element_type=F32)
        scores = qk * w_intra
        num = (w_inter * jnp.dot(q, c_prev.astype(BF16), preferred_element_type=F32)
               + jnp.dot(scores.astype(BF16), v, preferred_element_type=F32))
        den = (w_inter * jnp.sum(q.astype(F32) * n_prev, axis=-1, keepdims=True)
               + jnp.sum(scores, axis=-1, keepdims=True))
        hh = num / jnp.maximum(jnp.abs(den), jnp.exp(-m_row))

        b_last = b_col[L - 1:L, :]
        a_log = b_last - b_col + i_col
        m_new = jnp.maximum(b_last + m_prev, jnp.max(a_log, axis=0, keepdims=True))
        decay = jnp.exp(b_last + m_prev - m_new)
        w_upd = jnp.exp(a_log - m_new)
        kw = k.astype(F32) * w_upd
        c_ref[h] = decay * c_prev + jnp.dot(kw.T.astype(BF16), v, preferred_element_type=F32)
        n_ref[h] = decay * n_prev + jnp.sum(kw, axis=0, keepdims=True)
        m_ref[h] = jnp.broadcast_to(m_new, m_ref.shape[1:])

        ms = jnp.mean(hh * hh, axis=-1, keepdims=True)
        y = (hh * lax.rsqrt(ms + EPS) * gout_ref[:, vs]
             * jax.nn.sigmoid(mo_ref[0, :, vs].astype(F32)))
        o_ref[0, :, vs] = y.astype(o_ref.dtype)


def _mlstm(proj3d, gate_cols, gate_rows, gout, *, chunk):
    bsz, seq, _ = proj3d.shape
    return pl.pallas_call(
        _mlstm_kernel,
        grid=(bsz, seq // chunk),
        in_specs=[
            pl.BlockSpec((1, chunk, ML_QK_W), lambda b, c: (b, c, COL_MQ // ML_QK_W)),
            pl.BlockSpec((1, chunk, ML_QK_W), lambda b, c: (b, c, COL_MK // ML_QK_W)),
            pl.BlockSpec((1, chunk, ML_V_W), lambda b, c: (b, c, COL_MV // ML_V_W)),
            pl.BlockSpec((1, chunk, ML_V_W), lambda b, c: (b, c, COL_MO // ML_V_W)),
            pl.BlockSpec((1, chunk, LANES), lambda b, c: (b, c, 0)),
            pl.BlockSpec((1, 16, chunk), lambda b, c: (b, 0, c)),
            pl.BlockSpec((1, ML_V_W), lambda b, c: (0, 0)),
        ],
        out_specs=pl.BlockSpec((1, chunk, ML_V_W), lambda b, c: (b, c, 0)),
        out_shape=jax.ShapeDtypeStruct((bsz, seq, ML_V_W), BF16),
        scratch_shapes=[
            pltpu.VMEM((ML_HEADS, ML_QK_DIM, ML_V_DIM), F32),
            pltpu.VMEM((ML_HEADS, 1, ML_QK_DIM), F32),
            pltpu.VMEM((ML_HEADS, 8, LANES), F32),
        ],
        compiler_params=_cparams(2),
        name="mlstm",
    )(proj3d, proj3d, proj3d, proj3d, gate_cols, gate_rows, gout)


def _merge_kernel(ya_ref, yb_ref, ga_ref, gb_ref, x_ref, wa_ref, wb_ref, wo_ref, g2_ref,
                  wr_ref, br_ref, h_ref, tp_ref, route_ref, cnt_ref, carry_ref):
    i = pl.program_id(0)
    tm = x_ref.shape[0]

    @pl.when(i == 0)
    def _():
        carry_ref[...] = jnp.zeros_like(carry_ref)

    a = jnp.dot(ya_ref[...], wa_ref[...], preferred_element_type=F32)
    b = jnp.dot(yb_ref[...], wb_ref[...], preferred_element_type=F32)
    merged = (jax.nn.sigmoid(ga_ref[...].astype(F32)) * a
              + jax.nn.sigmoid(gb_ref[...].astype(F32)) * b)
    h = x_ref[...] + jnp.dot(merged.astype(BF16), wo_ref[...], preferred_element_type=F32)
    h_ref[...] = h
    t = _rms_norm_rows(h, g2_ref[...])
    tp_ref[...] = _pack_rows(t)

    t_hi = t.astype(BF16)
    t_lo = (t - t_hi.astype(F32)).astype(BF16)
    w_r = wr_ref[...]
    w_hi = w_r.astype(BF16)
    w_lo = (w_r - w_hi.astype(F32)).astype(BF16)
    logits = (jnp.dot(t_hi, w_hi, preferred_element_type=F32)
              + jnp.dot(t_lo, w_hi, preferred_element_type=F32)
              + jnp.dot(t_hi, w_lo, preferred_element_type=F32)
              + br_ref[...])
    lane = lax.broadcasted_iota(jnp.int32, (tm, LANES), 1).astype(F32)
    lg = logits
    sels, vals, idxs = [], [], []
    for _ in range(TOP_K):
        mx = jnp.max(lg, axis=-1, keepdims=True)
        ik = jnp.min(jnp.where(lg == mx, lane, float(LANES)), axis=-1, keepdims=True)
        sel = lane == ik
        sels.append(sel)
        vals.append(mx)
        idxs.append(ik)
        lg = jnp.where(sel, -jnp.inf, lg)
    exps = [jnp.exp(v - vals[0]) for v in vals]
    den = exps[0] + exps[1] + exps[2] + exps[3]
    mask = jnp.zeros((tm, LANES), F32)
    for sel in sels:
        mask = mask + jnp.where(sel, 1.0, 0.0)
    row = lax.broadcasted_iota(jnp.int32, (tm, tm), 0)
    col = lax.broadcasted_iota(jnp.int32, (tm, tm), 1)
    strict = jnp.where(col < row, 1.0, 0.0).astype(BF16)
    ranks = jnp.dot(strict, mask.astype(BF16), preferred_element_type=F32) + carry_ref[...]
    slab = jnp.zeros((tm, LANES), F32)
    for kk in range(TOP_K):
        rank_k = jnp.sum(jnp.where(sels[kk], ranks, 0.0), axis=-1, keepdims=True)
        slab = jnp.where(lane == float(kk), idxs[kk], slab)
        slab = jnp.where(lane == float(TOP_K + kk), rank_k, slab)
        slab = jnp.where(lane == float(2 * TOP_K + kk), exps[kk] / den, slab)
    route_ref[...] = slab
    new_carry = carry_ref[...] + jnp.sum(mask, axis=0, keepdims=True)
    carry_ref[...] = new_carry
    cnt_ref[...] = new_carry


def _merge(y_a, y_b, proj, x2d, w_a, w_b, w_o, g2, w_r, b_r, *, tm):
    t_rows = x2d.shape[0]
    const = lambda shape: pl.BlockSpec(shape, lambda i: (0, 0), pipeline_mode=pl.Buffered(1))
    return pl.pallas_call(
        _merge_kernel,
        grid=(t_rows // tm,),
        in_specs=[
            pl.BlockSpec((tm, FOX_W), lambda i: (i, 0)),
            pl.BlockSpec((tm, ML_V_W), lambda i: (i, 0)),
            pl.BlockSpec((tm, D_MODEL), lambda i: (i, COL_GA // D_MODEL)),
            pl.BlockSpec((tm, D_MODEL), lambda i: (i, COL_GB // D_MODEL)),
            pl.BlockSpec((tm, D_MODEL), lambda i: (i, 0)),
            const((FOX_W, D_MODEL)),
            const((ML_V_W, D_MODEL)),
            const((D_MODEL, D_MODEL)),
            const((1, D_MODEL)),
            const((D_MODEL, LANES)),
            const((1, LANES)),
        ],
        out_specs=[
            pl.BlockSpec((tm, D_MODEL), lambda i: (i, 0)),
            pl.BlockSpec((tm, HALF), lambda i: (i, 0)),
            pl.BlockSpec((tm, LANES), lambda i: (i, 0)),
            pl.BlockSpec((1, LANES), lambda i: (0, 0)),
        ],
        out_shape=[
            jax.ShapeDtypeStruct((t_rows, D_MODEL), F32),
            jax.ShapeDtypeStruct((t_rows, HALF), jnp.uint32),
            jax.ShapeDtypeStruct((t_rows, LANES), F32),
            jax.ShapeDtypeStruct((1, LANES), F32),
        ],
        scratch_shapes=[pltpu.VMEM((1, LANES), F32)],
        compiler_params=_cparams(1),
        name="merge_router",
    )(y_a, y_b, proj, proj, x2d, w_a, w_b, w_o, g2, w_r, b_r)


def _dispatch_kernel(pos_ref, zt_ref, tp_ref, xs_ref, stage_ref, zero_ref, sem, zsem, *, tr, tile):
    i = pl.program_id(0)
    n = pl.num_programs(0)

    @pl.when(i == 0)
    def _():
        zero_ref[...] = jnp.zeros_like(zero_ref)

        def zero_copy(z):
            start = pl.multiple_of(jnp.maximum(zt_ref[z], 0) * tile, tile)
            return pltpu.make_async_copy(zero_ref, xs_ref.at[pl.ds(start, tile), :], zsem)

        def start_body(z, carry):
            @pl.when(zt_ref[z] >= 0)
            def _():
                zero_copy(z).start()
            return carry
        lax.fori_loop(0, zt_ref.shape[0], start_body, 0)

        def wait_body(z, carry):
            @pl.when(zt_ref[z] >= 0)
            def _():
                zero_copy(z).wait()
            return carry
        lax.fori_loop(0, zt_ref.shape[0], wait_body, 0)

    def row_copy(s, r, p):
        return pltpu.make_async_copy(stage_ref.at[s, pl.ds(r, 1), :], xs_ref.at[pl.ds(p, 1), :], sem.at[s])

    slot = i % 2
    stage_ref[slot] = tp_ref[...]
    base = i * (tr * TOP_K)

    def issue(r, carry):
        for kk in range(TOP_K):
            row_copy(slot, r, pos_ref[base + r * TOP_K + kk]).start()
        return carry
    lax.fori_loop(0, tr, issue, 0, unroll=4)

    def drain(s):
        def body(r, carry):
            for kk in range(TOP_K):
                row_copy(s, r, 0).wait()
            return carry
        lax.fori_loop(0, tr, body, 0)

    @pl.when(i > 0)
    def _():
        drain(1 - slot)

    @pl.when(i == n - 1)
    def _():
        drain(slot)


def _dispatch(pos_flat, zero_tiles, tpk, *, n_rows, tr, tile):
    t_rows = tpk.shape[0]
    return pl.pallas_call(
        functools.partial(_dispatch_kernel, tr=tr, tile=tile),
        grid_spec=pltpu.PrefetchScalarGridSpec(
            num_scalar_prefetch=2,
            grid=(t_rows // tr,),
            in_specs=[pl.BlockSpec((tr, HALF), lambda i, pos, zt: (i, 0))],
            out_specs=pl.BlockSpec(memory_space=pl.ANY),
            scratch_shapes=[
                pltpu.VMEM((2, tr, HALF), jnp.uint32),
                pltpu.VMEM((tile, HALF), jnp.uint32),
                pltpu.SemaphoreType.DMA((2,)),
                pltpu.SemaphoreType.DMA(()),
            ],
        ),
        out_shape=jax.ShapeDtypeStruct((n_rows, HALF), jnp.uint32),
        compiler_params=_cparams(1),
        name="dispatch",
    )(pos_flat, zero_tiles, tpk)


def _experts_kernel(te_ref, nv_ref, nu_ref, xs_ref, wg_ref, bg_ref, wu_ref, bu_ref, wd_ref, bd_ref,
                    o_ref, xb_ref, acc_ref, *, sub):
    del te_ref, nu_ref
    i = pl.program_id(0)
    j = pl.program_id(1)
    n_ff = pl.num_programs(1)
    tm = xs_ref.shape[0]
    valid = nv_ref[i]

    @pl.when(j == 0)
    def _():
        lo, hi = _unpack_rows(xs_ref[...])
        xb_ref[:, :HALF] = lo.astype(BF16)
        xb_ref[:, HALF:] = hi.astype(BF16)
        acc_ref[...] = jnp.broadcast_to(bd_ref[0], acc_ref.shape)

    def ffn_chunk(n_rows):
        xb = xb_ref[0:n_rows, :]
        g = jnp.dot(xb, wg_ref[0].astype(BF16), preferred_element_type=F32) + bg_ref[0]
        u = jnp.dot(xb, wu_ref[0].astype(BF16), preferred_element_type=F32) + bu_ref[0]
        g = jnp.minimum(g, SWIGLU_LIMIT)
        u = jnp.clip(u, -SWIGLU_LIMIT, SWIGLU_LIMIT)
        act = (u + 1.0) * g * jax.nn.sigmoid(SWIGLU_ALPHA * g)
        acc_ref[0:n_rows, :] += jnp.dot(act.astype(BF16), wd_ref[0].astype(BF16),
                                        preferred_element_type=F32)

    n_sub = tm // sub
    for s in range(n_sub):
        lo_rows, hi_rows = s * sub, (s + 1) * sub
        upper = (valid <= hi_rows) if s + 1 < n_sub else True

        @pl.when((valid > lo_rows) & upper)
        def _():
            ffn_chunk(hi_rows)

    @pl.when(j == n_ff - 1)
    def _():
        o_ref[...] = _pack_rows(acc_ref[...])


def _experts(tile_expert, tile_valid, n_used, xs, w_gate, b_gate, w_up, b_up, w_down, b_down,
             *, tm, tf, sub):
    n_tiles = xs.shape[0] // tm
    n_ff = D_FF // tf

    def tile_i(i, nu):
        return jnp.maximum(jnp.minimum(i, nu[0] - 1), 0)

    def ff_j(i, j, nu):
        return jnp.where(i < nu[0], j, n_ff - 1)

    return pl.pallas_call(
        functools.partial(_experts_kernel, sub=sub),
        grid_spec=pltpu.PrefetchScalarGridSpec(
            num_scalar_prefetch=3,
            grid=(n_tiles, n_ff),
            in_specs=[
                pl.BlockSpec((tm, HALF), lambda i, j, te, nv, nu: (tile_i(i, nu), 0)),
                pl.BlockSpec((1, D_MODEL, tf), lambda i, j, te, nv, nu: (te[i], 0, ff_j(i, j, nu))),
                pl.BlockSpec((1, 1, tf), lambda i, j, te, nv, nu: (te[i], 0, ff_j(i, j, nu))),
                pl.BlockSpec((1, D_MODEL, tf), lambda i, j, te, nv, nu: (te[i], 0, ff_j(i, j, nu))),
                pl.BlockSpec((1, 1, tf), lambda i, j, te, nv, nu: (te[i], 0, ff_j(i, j, nu))),
                pl.BlockSpec((1, tf, D_MODEL), lambda i, j, te, nv, nu: (te[i], ff_j(i, j, nu), 0)),
                pl.BlockSpec((1, 1, D_MODEL), lambda i, j, te, nv, nu: (te[i], 0, 0)),
            ],
            out_specs=pl.BlockSpec((tm, HALF), lambda i, j, te, nv, nu: (i, 0)),
            scratch_shapes=[
                pltpu.VMEM((tm, D_MODEL), BF16),
                pltpu.VMEM((tm, D_MODEL), F32),
            ],
        ),
        out_shape=jax.ShapeDtypeStruct(xs.shape, jnp.uint32),
        compiler_params=_cparams(2),
        name="experts",
    )(tile_expert, tile_valid, n_used, xs, w_gate, b_gate, w_up, b_up, w_down, b_down)


def _combine_kernel(pos_ref, h_ref, w_ref, ys_ref, o_ref, buf_ref, sem):
    i = pl.program_id(0)
    n = pl.num_programs(0)
    tc = h_ref.shape[0]

    def row_copy(p, slot, kk, r):
        return pltpu.make_async_copy(ys_ref.at[pl.ds(p, 1), :],
                                     buf_ref.at[slot, kk, pl.ds(r, 1), :], sem.at[slot])

    def issue(step, slot):
        base = step * (tc * TOP_K)

        def body(r, carry):
            for kk in range(TOP_K):
                row_copy(pos_ref[base + r * TOP_K + kk], slot, kk, r).start()
            return carry
        lax.fori_loop(0, tc, body, 0, unroll=4)

    @pl.when(i == 0)
    def _():
        issue(0, 0)

    @pl.when(i + 1 < n)
    def _():
        issue(i + 1, (i + 1) % 2)

    slot = i % 2

    def drain(r, carry):
        for kk in range(TOP_K):
            row_copy(0, slot, kk, r).wait()
        return carry
    lax.fori_loop(0, tc, drain, 0)

    w = w_ref[...]
    acc_lo = h_ref[:, :HALF]
    acc_hi = h_ref[:, HALF:]
    for kk in range(TOP_K):
        lo, hi = _unpack_rows(buf_ref[slot, kk])
        acc_lo = acc_lo + w[:, kk:kk + 1] * lo
        acc_hi = acc_hi + w[:, kk:kk + 1] * hi
    o_ref[:, :HALF] = acc_lo
    o_ref[:, HALF:] = acc_hi


def _combine(pos_flat, h2d, w_top, ys, *, tc):
    t_rows = h2d.shape[0]
    return pl.pallas_call(
        _combine_kernel,
        grid_spec=pltpu.PrefetchScalarGridSpec(
            num_scalar_prefetch=1,
            grid=(t_rows // tc,),
            in_specs=[
                pl.BlockSpec((tc, D_MODEL), lambda i, pos: (i, 0)),
                pl.BlockSpec((tc, TOP_K), lambda i, pos: (i, 0)),
                pl.BlockSpec(memory_space=pl.ANY),
            ],
            out_specs=pl.BlockSpec((tc, D_MODEL), lambda i, pos: (i, 0)),
            scratch_shapes=[
                pltpu.VMEM((2, TOP_K, tc, HALF), jnp.uint32),
                pltpu.SemaphoreType.DMA((2,)),
            ],
        ),
        out_shape=jax.ShapeDtypeStruct(h2d.shape, F32),
        compiler_params=_cparams(1),
        name="combine",
    )(pos_flat, h2d, w_top, ys)


def _pick(n, pref):
    t = min(n, pref)
    assert n % t == 0, (n, t)
    return t


def kernel(x, norm1_g, w_in, fox_f_bias, q_norm_g, k_norm_g, ml_i_bias, ml_f_bias, ml_out_norm_g,
           w_branch_a, w_branch_b, w_out, norm2_g, w_router, b_router,
           w_gate, b_gate, w_up, b_up, w_down, b_down):
    bsz, seq, d_model = x.shape
    assert d_model == D_MODEL and norm1_g.shape[0] == 1, "single-layer block of width 2048"
    t_rows = bsz * seq
    x2d = x.reshape(t_rows, D_MODEL).astype(F32)

    offs = [0]
    for wdt in IN_WIDTHS:
        offs.append(offs[-1] + wdt)
    seg = lambda n: w_in[0][:, offs[n]:offs[n + 1]]
    w_main = jnp.zeros((D_MODEL, PROJ_W), BF16)
    for src, dst in ((offs[0], COL_FQ), (offs[4], COL_MQ), (offs[9], COL_MO)):
        width = {COL_FQ: COL_MQ - COL_FQ, COL_MQ: COL_MO - COL_MQ, COL_MO: PROJ_W - COL_MO}[dst]
        w_main = lax.dynamic_update_slice(w_main, w_in[0][:, src:src + width].astype(BF16), (0, dst))
    w_gl = jnp.concatenate([seg(3), seg(7), seg(8),
                            jnp.zeros((D_MODEL, LANES - FOX_HEADS - 2 * ML_HEADS), F32)], axis=1).astype(BF16)
    colscale = jnp.ones((PROJ_W,), F32)
    colscale = colscale.at[COL_FQ:COL_FQ + FOX_W].set(
        jnp.tile(q_norm_g[0].astype(F32), FOX_HEADS) * (FOX_HEAD_DIM ** -0.5 * LOG2_E))
    colscale = colscale.at[COL_FK:COL_FK + FOX_W].set(jnp.tile(k_norm_g[0].astype(F32), FOX_HEADS))
    colscale = colscale.at[COL_MK:COL_MK + ML_QK_W].set(ML_QK_DIM ** -0.5)
    colscale = colscale.reshape(1, PROJ_W)
    gate_bias = jnp.concatenate([fox_f_bias[0], ml_i_bias[0], ml_f_bias[0],
                                 jnp.zeros((LANES - FOX_HEADS - 2 * ML_HEADS,), F32)]).astype(F32).reshape(1, LANES)
    g2 = norm2_g.astype(F32).reshape(1, D_MODEL)

    proj, gates = _in_proj(x2d, norm1_g.astype(F32).reshape(1, D_MODEL), w_main, w_gl, colscale,
                           tm=_pick(t_rows, 1024), tn=512)
    chunk = _pick(seq, 256)
    gp = _gate_prep(gates.reshape(bsz, seq, LANES), gate_bias, ts=_pick(seq, 512), chunk=chunk)
    gp_rows = jnp.transpose(gp[:, :, :16], (0, 2, 1))
    fox_group = 2
    neg_cum_f = (-LOG2_E * gp_rows[:, GCOL_FF:GCOL_FF + FOX_HEADS, :]).reshape(
        bsz * FOX_HEADS // fox_group, fox_group, seq)
    proj3d = proj.reshape(bsz, seq, PROJ_W)
    qk_bound = (1.02 * FOX_HEAD_DIM * (FOX_HEAD_DIM ** -0.5 * LOG2_E)
                * jnp.max(jnp.abs(q_norm_g[0].astype(F32))) * jnp.max(jnp.abs(k_norm_g[0].astype(F32))))
    y_a = _fox_attention(proj3d, neg_cum_f, qk_bound, tq=_pick(seq, 512), heads=fox_group)
    y_b = _mlstm(proj3d, gp, gp_rows, ml_out_norm_g.astype(F32).reshape(1, ML_V_W), chunk=chunk)

    w_r = jnp.concatenate([w_router[0].astype(F32), jnp.zeros((D_MODEL, LANES - N_EXPERTS), F32)], axis=1)
    b_r = jnp.concatenate([b_router[0].astype(F32), jnp.full((LANES - N_EXPERTS,), -jnp.inf, F32)]).reshape(1, LANES)
    h2d, tpk, route, counts = _merge(
        y_a.reshape(t_rows, FOX_W), y_b.reshape(t_rows, ML_V_W), proj, x2d,
        w_branch_a[0].astype(BF16), w_branch_b[0].astype(BF16), w_out[0].astype(BF16),
        g2, w_r, b_r, tm=_pick(t_rows, 512))

    tm_e = _pick(t_rows * TOP_K, 1024)
    n_tiles = (t_rows * TOP_K) // tm_e + N_EXPERTS
    top_idx = route[:, 0:TOP_K].astype(jnp.int32)
    top_rank = route[:, TOP_K:2 * TOP_K].astype(jnp.int32)
    top_w = route[:, 2 * TOP_K:3 * TOP_K]
    cnt = counts[0, :N_EXPERTS].astype(jnp.int32)
    tiles_per_e = (cnt + tm_e - 1) // tm_e
    tile_end = jnp.cumsum(tiles_per_e)
    tile_begin = tile_end - tiles_per_e
    pos_flat = ((tile_begin * tm_e)[top_idx] + top_rank).reshape(-1)
    n_used = tile_end[-1:].astype(jnp.int32)
    tile_ids = jnp.arange(n_tiles, dtype=jnp.int32)
    used_ids = jnp.minimum(tile_ids, n_used[0] - 1)
    tile_expert = jnp.sum((used_ids[:, None] >= tile_end[None, :]).astype(jnp.int32), axis=1)
    tile_valid = jnp.clip(cnt[tile_expert] - (used_ids - tile_begin[tile_expert]) * tm_e, 0, tm_e)
    tile_valid = jnp.where(tile_ids < n_used[0], tile_valid, 0).astype(jnp.int32)

    last_tile = jnp.where(tiles_per_e > 0, tile_end - 1, -1)
    tail_ids = tile_ids[n_tiles - N_EXPERTS:]
    zero_tiles = jnp.concatenate([last_tile, jnp.where(tail_ids >= n_used[0], tail_ids, -1)]).astype(jnp.int32)
    xs = _dispatch(pos_flat, zero_tiles, tpk, n_rows=n_tiles * tm_e, tr=_pick(t_rows, 256), tile=tm_e)
    ys = _experts(tile_expert, tile_valid, n_used, xs,
                  w_gate[0], b_gate[0].reshape(N_EXPERTS, 1, D_FF),
                  w_up[0], b_up[0].reshape(N_EXPERTS, 1, D_FF),
                  w_down[0], b_down[0].reshape(N_EXPERTS, 1, D_MODEL),
                  tm=tm_e, tf=256, sub=_pick(tm_e, 256))
    out = _combine(pos_flat, h2d, top_w, ys, tc=_pick(t_rows, 128))
    return out.reshape(bsz, seq, D_MODEL).astype(x.dtype)
```

```python
import functools

import jax
import jax.numpy as jnp
from jax import lax
from jax.experimental import pallas as pl
from jax.experimental.pallas import tpu as pltpu

F32 = jnp.float32
BF16 = jnp.bfloat16

D_MODEL = 2048
FOX_HEADS = 8
FOX_HEAD_DIM = 128
FOX_W = FOX_HEADS * FOX_HEAD_DIM
ML_HEADS = 4
ML_QK_DIM = 128
ML_V_DIM = 256
ML_QK_W = ML_HEADS * ML_QK_DIM
ML_V_W = ML_HEADS * ML_V_DIM
IGATE_CAP = 15.0
N_EXPERTS = 32
TOP_K = 4
D_FF = 2048
SWIGLU_LIMIT = 7.0
SWIGLU_ALPHA = 1.702
EPS = 1e-5

IN_WIDTHS = (FOX_W, FOX_W, FOX_W, FOX_HEADS,
             ML_QK_W, ML_QK_W, ML_V_W, ML_HEADS, ML_HEADS, ML_V_W,
             D_MODEL, D_MODEL)

LANES = 128
VMEM_LIMIT = 56 * 1024 * 1024

COL_FQ, COL_FK, COL_FV = 0, FOX_W, 2 * FOX_W
COL_MQ = 3 * FOX_W
COL_MK = COL_MQ + ML_QK_W
COL_MV = COL_MK + ML_QK_W
COL_MO = COL_MV + ML_V_W
COL_GA = COL_MO + ML_V_W
COL_GB = COL_GA + D_MODEL
PROJ_W = COL_GB + D_MODEL
GCOL_FF, GCOL_MI, GCOL_MF = 0, FOX_HEADS, FOX_HEADS + ML_HEADS

HALF = D_MODEL // 2
LOG2_E = 1.4426950408889634


def _cparams(n_axes, vmem=VMEM_LIMIT):
    return pltpu.CompilerParams(dimension_semantics=("arbitrary",) * n_axes,
                                vmem_limit_bytes=vmem)


def _log_sigmoid(x):
    return jnp.minimum(x, 0.0) - jnp.log1p(jnp.exp(-jnp.abs(x)))


def _pack_rows(x):
    lo = lax.bitcast_convert_type(x[:, :HALF].astype(BF16).astype(F32), jnp.uint32)
    hi = lax.bitcast_convert_type(x[:, HALF:].astype(BF16).astype(F32), jnp.uint32)
    return hi | (lo >> 16)


def _unpack_rows(p):
    lo = lax.bitcast_convert_type(p << 16, F32)
    hi = lax.bitcast_convert_type(p & jnp.uint32(0xFFFF0000), F32)
    return lo, hi


def _rms_norm_rows(h, g):
    ms = jnp.mean(h * h, axis=-1, keepdims=True)
    return h * lax.rsqrt(ms + EPS) * g


def _repack_kernel(a_ref, b_ref, o_ref, *, shifts, row_chunk):
    j = pl.program_id(0)
    rows = a_ref.shape[0]
    for shift in sorted(set(shifts)):
        lo = min(jb for jb, sh in enumerate(shifts) if sh == shift)
        hi = max(jb for jb, sh in enumerate(shifts) if sh == shift)

        @pl.when((j >= lo) & (j <= hi))
        def _():
            def body(c, carry):
                r0 = pl.multiple_of(c * row_chunk, row_chunk)
                a = a_ref[pl.ds(r0, row_chunk), :]
                if shift:
                    a = jnp.concatenate([a[:, shift:], b_ref[pl.ds(r0, row_chunk), :shift]], axis=1)
                o_ref[pl.ds(r0, row_chunk), :] = a.astype(o_ref.dtype)
                return carry
            lax.fori_loop(0, rows // row_chunk, body, 0)


def _repack_w_in(w2d, *, tn):
    n_blocks = PROJ_W // tn
    gate_a, gate_b = FOX_HEADS, 2 * ML_HEADS
    shifts = tuple(0 if jb * tn < COL_MQ else (gate_a if jb * tn < COL_MO else gate_a + gate_b)
                   for jb in range(n_blocks))
    per = tn // LANES
    return pl.pallas_call(
        functools.partial(_repack_kernel, shifts=shifts, row_chunk=256),
        grid=(n_blocks,),
        in_specs=[
            pl.BlockSpec((D_MODEL, tn), lambda j: (0, j)),
            pl.BlockSpec((D_MODEL, LANES), lambda j: (0, (j + 1) * per)),
        ],
        out_specs=pl.BlockSpec((D_MODEL, tn), lambda j: (0, j)),
        out_shape=jax.ShapeDtypeStruct((D_MODEL, PROJ_W), BF16),
        compiler_params=_cparams(1),
        name="repack_w_in",
    )(w2d, w2d)


def _in_proj_kernel(x_ref, g1_ref, w_ref, wg_ref, cs_ref, o_ref, gate_ref, xn_ref,
                    *, n_norm_blocks, row_chunk):
    j = pl.program_id(1)
    tm = x_ref.shape[0]
    tn = w_ref.shape[1]

    @pl.when(j == 0)
    def _():
        def body(c, carry):
            r0 = pl.multiple_of(c * row_chunk, row_chunk)
            x = x_ref[pl.ds(r0, row_chunk), :]
            ms = jnp.mean(x * x, axis=-1, keepdims=True)
            xn_ref[pl.ds(r0, row_chunk), :] = (x * lax.rsqrt(ms + EPS) * g1_ref[...]).astype(BF16)
            return carry
        lax.fori_loop(0, tm // row_chunk, body, 0)
        gate_ref[...] = jnp.dot(xn_ref[...], wg_ref[...], preferred_element_type=F32)

    @pl.when(j < n_norm_blocks)
    def _():
        half = tn // 2
        for c0 in (0, half):
            acc = jnp.dot(xn_ref[...], w_ref[:, c0:c0 + half], preferred_element_type=F32)
            for s in range(half // LANES):
                a = acc[:, s * LANES:(s + 1) * LANES]
                cols = slice(c0 + s * LANES, c0 + (s + 1) * LANES)
                ms = jnp.mean(a * a, axis=-1, keepdims=True)
                o_ref[:, cols] = (a * lax.rsqrt(ms + EPS) * cs_ref[:, cols]).astype(o_ref.dtype)

    @pl.when(j >= n_norm_blocks)
    def _():
        half = tn // 2
        for c0 in (0, half):
            acc = jnp.dot(xn_ref[...], w_ref[:, c0:c0 + half], preferred_element_type=F32)
            o_ref[:, c0:c0 + half] = (acc * cs_ref[:, c0:c0 + half]).astype(o_ref.dtype)


def _in_proj(x2d, g1, w_main, w_gate, colscale, *, tm, tn):
    t_rows = x2d.shape[0]
    grid = (t_rows // tm, PROJ_W // tn)
    kern = functools.partial(_in_proj_kernel, n_norm_blocks=(2 * FOX_W) // tn, row_chunk=128)
    return pl.pallas_call(
        kern,
        grid=grid,
        in_specs=[
            pl.BlockSpec((tm, D_MODEL), lambda i, j: (i, 0)),
            pl.BlockSpec((1, D_MODEL), lambda i, j: (0, 0)),
            pl.BlockSpec((D_MODEL, tn), lambda i, j: (0, j)),
            pl.BlockSpec((D_MODEL, LANES), lambda i, j: (0, 0)),
            pl.BlockSpec((1, tn), lambda i, j: (0, j)),
        ],
        out_specs=[
            pl.BlockSpec((tm, tn), lambda i, j: (i, j)),
            pl.BlockSpec((tm, LANES), lambda i, j: (i, 0)),
        ],
        out_shape=[
            jax.ShapeDtypeStruct((t_rows, PROJ_W), BF16),
            jax.ShapeDtypeStruct((t_rows, LANES), F32),
        ],
        scratch_shapes=[pltpu.VMEM((tm, D_MODEL), BF16)],
        compiler_params=_cparams(2),
        name="in_proj",
    )(x2d, g1, w_main, w_gate, colscale)


def _gate_prep_kernel(g_ref, b_ref, o_ref, carry_ref, *, chunk):
    s = pl.program_id(1)
    ts = g_ref.shape[1]

    @pl.when(s == 0)
    def _():
        carry_ref[...] = jnp.zeros_like(carry_ref)

    z = g_ref[0] + b_ref[...]
    log_f = _log_sigmoid(z)
    i_pre = IGATE_CAP * jnp.tanh(z / IGATE_CAP)
    row = lax.broadcasted_iota(jnp.int32, (ts, ts), 0)
    col = lax.broadcasted_iota(jnp.int32, (ts, ts), 1)
    tril = (col <= row)
    same_chunk = (row // chunk) == (col // chunk)
    tril_f = jnp.where(tril, 1.0, 0.0).astype(F32)
    tril_c = jnp.where(tril & same_chunk, 1.0, 0.0).astype(F32)
    run_sum = jnp.dot(tril_f, log_f, preferred_element_type=F32,
                      precision=lax.Precision.HIGHEST) + carry_ref[...]
    chunk_sum = jnp.dot(tril_c, log_f, preferred_element_type=F32,
                        precision=lax.Precision.HIGHEST)
    carry_ref[...] = run_sum[ts - 1:ts, :]
    lane = lax.broadcasted_iota(jnp.int32, (ts, LANES), 1)
    o_ref[0] = jnp.where(lane < GCOL_MI, run_sum, jnp.where(lane < GCOL_MF, i_pre, chunk_sum))


def _gate_prep(gates3d, bias, *, ts, chunk):
    bsz, seq, _ = gates3d.shape
    return pl.pallas_call(
        functools.partial(_gate_prep_kernel, chunk=chunk),
        grid=(bsz, seq // ts),
        in_specs=[
            pl.BlockSpec((1, ts, LANES), lambda b, s: (b, s, 0)),
            pl.BlockSpec((1, LANES), lambda b, s: (0, 0)),
        ],
        out_specs=pl.BlockSpec((1, ts, LANES), lambda b, s: (b, s, 0)),
        out_shape=jax.ShapeDtypeStruct((bsz, seq, LANES), F32),
        scratch_shapes=[pltpu.VMEM((1, LANES), F32)],
        compiler_params=_cparams(2),
        name="gate_prep",
    )(gates3d, bias)


def _fox_kernel(lo_ref, q_ref, k_ref, v_ref, ncf_ref, o_ref, *, tq):
    qi = pl.program_id(2)
    first = lo_ref[(pl.program_id(0) * pl.num_programs(1) + pl.program_id(1)) * pl.num_programs(2) + qi]
    hd = FOX_HEAD_DIM
    heads = q_ref.shape[-1] // hd

    def block(kb, carry, masked):
        off = pl.multiple_of(kb * tq, tq)
        new = []
        for hh in range(heads):
            cols = slice(hh * hd, (hh + 1) * hd)
            m, l, acc = carry[hh]
            q = q_ref[0, :, cols]
            k = k_ref[0, pl.ds(off, tq), cols]
            v = v_ref[0, pl.ds(off, tq), cols]
            s = lax.dot_general(q, k, (((1,), (1,)), ((), ())), preferred_element_type=F32)
            s = s + ncf_ref[0, hh:hh + 1, pl.ds(off, tq)]
            if masked:
                row = lax.broadcasted_iota(jnp.int32, (tq, tq), 0)
                col = lax.broadcasted_iota(jnp.int32, (tq, tq), 1)
                s = jnp.where(row >= col, s, -jnp.inf)
            m_new = jnp.maximum(m, jnp.max(s, axis=-1, keepdims=True))
            alpha = jnp.exp2(m - m_new)
            p = jnp.exp2(s - m_new)
            l = alpha * l + jnp.sum(p, axis=-1, keepdims=True)
            acc = alpha * acc + jnp.dot(p.astype(BF16), v, preferred_element_type=F32)
            new.append((m_new, l, acc))
        return tuple(new)

    init = tuple((jnp.full((tq, 1), -jnp.inf, F32), jnp.zeros((tq, 1), F32), jnp.zeros((tq, hd), F32))
                 for _ in range(heads))
    carry = lax.fori_loop(first, qi, lambda kb, c: block(kb, c, False), init)
    final = block(qi, carry, True)
    for hh in range(heads):
        _, l, acc = final[hh]
        o_ref[0, :, hh * hd:(hh + 1) * hd] = (acc / l).astype(o_ref.dtype)


FOX_PRUNE_MARGIN = 150.0


def _fox_first_block(neg_cum_f, qk_bound, *, tq, heads):
    n_seq, _, seq = neg_cum_f.shape[0], neg_cum_f.shape[1], neg_cum_f.shape[2]
    nq = seq // tq
    blk = neg_cum_f.reshape(n_seq, heads, nq, tq)
    blk_max = lax.cummax(jnp.max(blk, axis=-1), axis=2)
    row_min = jnp.min(blk, axis=-1)
    thresh = row_min - (2.0 * qk_bound + FOX_PRUNE_MARGIN)
    skippable = blk_max[:, :, None, :] < thresh[:, :, :, None]
    first = jnp.sum(skippable.astype(jnp.int32), axis=-1)
    first = jnp.min(first, axis=1)
    first = jnp.minimum(first, jnp.arange(nq, dtype=jnp.int32)[None, :])
    return first.reshape(-1).astype(jnp.int32)


def _fox_attention(proj3d, neg_cum_f, qk_bound, *, tq, heads):
    bsz, seq, _ = proj3d.shape
    hw = heads * FOX_HEAD_DIM
    groups = FOX_HEADS // heads
    first_block = _fox_first_block(neg_cum_f, qk_bound, tq=tq, heads=heads)
    return pl.pallas_call(
        functools.partial(_fox_kernel, tq=tq),
        grid_spec=pltpu.PrefetchScalarGridSpec(
            num_scalar_prefetch=1,
            grid=(bsz, groups, seq // tq),
            in_specs=[
                pl.BlockSpec((1, tq, hw), lambda b, g, i, lo: (b, i, COL_FQ // hw + g)),
                pl.BlockSpec((1, seq, hw), lambda b, g, i, lo: (b, 0, COL_FK // hw + g)),
                pl.BlockSpec((1, seq, hw), lambda b, g, i, lo: (b, 0, COL_FV // hw + g)),
                pl.BlockSpec((1, heads, seq), lambda b, g, i, lo: (b * groups + g, 0, 0)),
            ],
            out_specs=pl.BlockSpec((1, tq, hw), lambda b, g, i, lo: (b, i, g)),
        ),
        out_shape=jax.ShapeDtypeStruct((bsz, seq, FOX_W), BF16),
        compiler_params=_cparams(3),
        name="fox_attn",
    )(first_block, proj3d, proj3d, proj3d, neg_cum_f)


def _mlstm_kernel(q_ref, k_ref, v_ref, mo_ref, gc_ref, gr_ref, gout_ref, o_ref,
                  c_ref, n_ref, m_ref):
    c_idx = pl.program_id(1)
    L = q_ref.shape[1]

    @pl.when(c_idx == 0)
    def _():
        c_ref[...] = jnp.zeros_like(c_ref)
        n_ref[...] = jnp.zeros_like(n_ref)
        m_ref[...] = jnp.zeros_like(m_ref)

    row = lax.broadcasted_iota(jnp.int32, (L, L), 0)
    col = lax.broadcasted_iota(jnp.int32, (L, L), 1)
    tril = col <= row

    for h in range(ML_HEADS):
        qs = slice(h * ML_QK_DIM, (h + 1) * ML_QK_DIM)
        vs = slice(h * ML_V_DIM, (h + 1) * ML_V_DIM)
        q = q_ref[0, :, qs]
        k = k_ref[0, :, qs]
        v = v_ref[0, :, vs]
        b_col = gc_ref[0, :, GCOL_MF + h:GCOL_MF + h + 1]
        i_col = gc_ref[0, :, GCOL_MI + h:GCOL_MI + h + 1]
        b_row = gr_ref[0, GCOL_MF + h:GCOL_MF + h + 1, :]
        i_row = gr_ref[0, GCOL_MI + h:GCOL_MI + h + 1, :]
        m_prev = m_ref[h, 0:1, 0:1]
        c_prev = c_ref[h]
        n_prev = n_ref[h]

        d_log = jnp.where(tril, b_col - b_row + i_row, -jnp.inf)
        g_inter = b_col + m_prev
        m_row = jnp.maximum(g_inter, jnp.max(d_log, axis=-1, keepdims=True))
        w_intra = jnp.exp(d_log - m_row)
        w_inter = jnp.exp(g_inter - m_row)
        qk = lax.dot_general(q, k, (((1,), (1,)), ((), ())), preferred_element_type=F32)
        scores = qk * w_intra
        num = (w_inter * jnp.dot(q, c_prev.astype(BF16), preferred_element_type=F32)
               + jnp.dot(scores.astype(BF16), v, preferred_element_type=F32))
        den = (w_inter * jnp.sum(q.astype(F32) * n_prev, axis=-1, keepdims=True)
               + jnp.sum(scores, axis=-1, keepdims=True))
        hh = num / jnp.maximum(jnp.abs(den), jnp.exp(-m_row))

        b_last = b_col[L - 1:L, :]
        a_log = b_last - b_col + i_col
        m_new = jnp.maximum(b_last + m_prev, jnp.max(a_log, axis=0, keepdims=True))
        decay = jnp.exp(b_last + m_prev - m_new)
        w_upd = jnp.exp(a_log - m_new)
        kw = k.astype(F32) * w_upd
        c_ref[h] = decay * c_prev + jnp.dot(kw.T.astype(BF16), v, preferred_element_type=F32)
        n_ref[h] = decay * n_prev + jnp.sum(kw, axis=0, keepdims=True)
        m_ref[h] = jnp.broadcast_to(m_new, m_ref.shape[1:])

        ms = jnp.mean(hh * hh, axis=-1, keepdims=True)
        y = (hh * lax.rsqrt(ms + EPS) * gout_ref[:, vs]
             * jax.nn.sigmoid(mo_ref[0, :, vs].astype(F32)))
        o_ref[0, :, vs] = y.astype(o_ref.dtype)


def _mlstm(proj3d, gate_cols, gate_rows, gout, *, chunk):
    bsz, seq, _ = proj3d.shape
    return pl.pallas_call(
        _mlstm_kernel,
        grid=(bsz, seq // chunk),
        in_specs=[
            pl.BlockSpec((1, chunk, ML_QK_W), lambda b, c: (b, c, COL_MQ // ML_QK_W)),
            pl.BlockSpec((1, chunk, ML_QK_W), lambda b, c: (b, c, COL_MK // ML_QK_W)),
            pl.BlockSpec((1, chunk, ML_V_W), lambda b, c: (b, c, COL_MV // ML_V_W)),
            pl.BlockSpec((1, chunk, ML_V_W), lambda b, c: (b, c, COL_MO // ML_V_W)),
            pl.BlockSpec((1, chunk, LANES), lambda b, c: (b, c, 0)),
            pl.BlockSpec((1, 16, chunk), lambda b, c: (b, 0, c)),
            pl.BlockSpec((1, ML_V_W), lambda b, c: (0, 0)),
        ],
        out_specs=pl.BlockSpec((1, chunk, ML_V_W), lambda b, c: (b, c, 0)),
        out_shape=jax.ShapeDtypeStruct((bsz, seq, ML_V_W), BF16),
        scratch_shapes=[
            pltpu.VMEM((ML_HEADS, ML_QK_DIM, ML_V_DIM), F32),
            pltpu.VMEM((ML_HEADS, 1, ML_QK_DIM), F32),
            pltpu.VMEM((ML_HEADS, 8, LANES), F32),
        ],
        compiler_params=_cparams(2),
        name="mlstm",
    )(proj3d, proj3d, proj3d, proj3d, gate_cols, gate_rows, gout)


def _merge_kernel(ya_ref, yb_ref, ga_ref, gb_ref, x_ref, wa_ref, wb_ref, wo_ref, g2_ref,
                  wr_ref, br_ref, h_ref, tp_ref, route_ref, cnt_ref, carry_ref):
    i = pl.program_id(0)
    tm = x_ref.shape[0]

    @pl.when(i == 0)
    def _():
        carry_ref[...] = jnp.zeros_like(carry_ref)

    a = jnp.dot(ya_ref[...], wa_ref[...], preferred_element_type=F32)
    b = jnp.dot(yb_ref[...], wb_ref[...], preferred_element_type=F32)
    merged = (jax.nn.sigmoid(ga_ref[...].astype(F32)) * a
              + jax.nn.sigmoid(gb_ref[...].astype(F32)) * b)
    h = x_ref[...] + jnp.dot(merged.astype(BF16), wo_ref[...], preferred_element_type=F32)
    h_ref[...] = h
    t = _rms_norm_rows(h, g2_ref[...])
    tp_ref[...] = _pack_rows(t)

    t_hi = t.astype(BF16)
    t_lo = (t - t_hi.astype(F32)).astype(BF16)
    w_r = wr_ref[...]
    w_hi = w_r.astype(BF16)
    w_lo = (w_r - w_hi.astype(F32)).astype(BF16)
    logits = (jnp.dot(t_hi, w_hi, preferred_element_type=F32)
              + jnp.dot(t_lo, w_hi, preferred_element_type=F32)
              + jnp.dot(t_hi, w_lo, preferred_element_type=F32)
              + br_ref[...])
    lane = lax.broadcasted_iota(jnp.int32, (tm, LANES), 1).astype(F32)
    lg = logits
    sels, vals, idxs = [], [], []
    for _ in range(TOP_K):
        mx = jnp.max(lg, axis=-1, keepdims=True)
        ik = jnp.min(jnp.where(lg == mx, lane, float(LANES)), axis=-1, keepdims=True)
        sel = lane == ik
        sels.append(sel)
        vals.append(mx)
        idxs.append(ik)
        lg = jnp.where(sel, -jnp.inf, lg)
    exps = [jnp.exp(v - vals[0]) for v in vals]
    den = exps[0] + exps[1] + exps[2] + exps[3]
    mask = jnp.zeros((tm, LANES), F32)
    for sel in sels:
        mask = mask + jnp.where(sel, 1.0, 0.0)
    row = lax.broadcasted_iota(jnp.int32, (tm, tm), 0)
    col = lax.broadcasted_iota(jnp.int32, (tm, tm), 1)
    strict = jnp.where(col < row, 1.0, 0.0).astype(BF16)
    ranks = jnp.dot(strict, mask.astype(BF16), preferred_element_type=F32) + carry_ref[...]
    slab = jnp.zeros((tm, LANES), F32)
    for kk in range(TOP_K):
        rank_k = jnp.sum(jnp.where(sels[kk], ranks, 0.0), axis=-1, keepdims=True)
        slab = jnp.where(lane == float(kk), idxs[kk], slab)
        slab = jnp.where(lane == float(TOP_K + kk), rank_k, slab)
        slab = jnp.where(lane == float(2 * TOP_K + kk), exps[kk] / den, slab)
    route_ref[...] = slab
    new_carry = carry_ref[...] + jnp.sum(mask, axis=0, keepdims=True)
    carry_ref[...] = new_carry
    cnt_ref[...] = new_carry


def _merge(y_a, y_b, proj, x2d, w_a, w_b, w_o, g2, w_r, b_r, *, tm):
    t_rows = x2d.shape[0]
    const = lambda shape: pl.BlockSpec(shape, lambda i: (0, 0), pipeline_mode=pl.Buffered(1))
    return pl.pallas_call(
        _merge_kernel,
        grid=(t_rows // tm,),
        in_specs=[
            pl.BlockSpec((tm, FOX_W), lambda i: (i, 0)),
            pl.BlockSpec((tm, ML_V_W), lambda i: (i, 0)),
            pl.BlockSpec((tm, D_MODEL), lambda i: (i, COL_GA // D_MODEL)),
            pl.BlockSpec((tm, D_MODEL), lambda i: (i, COL_GB // D_MODEL)),
            pl.BlockSpec((tm, D_MODEL), lambda i: (i, 0)),
            const((FOX_W, D_MODEL)),
            const((ML_V_W, D_MODEL)),
            const((D_MODEL, D_MODEL)),
            const((1, D_MODEL)),
            const((D_MODEL, LANES)),
            const((1, LANES)),
        ],
        out_specs=[
            pl.BlockSpec((tm, D_MODEL), lambda i: (i, 0)),
            pl.BlockSpec((tm, HALF), lambda i: (i, 0)),
            pl.BlockSpec((tm, LANES), lambda i: (i, 0)),
            pl.BlockSpec((1, LANES), lambda i: (0, 0)),
        ],
        out_shape=[
            jax.ShapeDtypeStruct((t_rows, D_MODEL), F32),
            jax.ShapeDtypeStruct((t_rows, HALF), jnp.uint32),
            jax.ShapeDtypeStruct((t_rows, LANES), F32),
            jax.ShapeDtypeStruct((1, LANES), F32),
        ],
        scratch_shapes=[pltpu.VMEM((1, LANES), F32)],
        compiler_params=_cparams(1),
        name="merge_router",
    )(y_a, y_b, proj, proj, x2d, w_a, w_b, w_o, g2, w_r, b_r)


def _dispatch_kernel(pos_ref, zt_ref, tp_ref, xs_ref, stage_ref, zero_ref, sem, zsem, *, tr, tile):
    i = pl.program_id(0)
    n = pl.num_programs(0)

    @pl.when(i == 0)
    def _():
        zero_ref[...] = jnp.zeros_like(zero_ref)

        def zero_copy(z):
            start = pl.multiple_of(jnp.maximum(zt_ref[z], 0) * tile, tile)
            return pltpu.make_async_copy(zero_ref, xs_ref.at[pl.ds(start, tile), :], zsem)

        def start_body(z, carry):
            @pl.when(zt_ref[z] >= 0)
            def _():
                zero_copy(z).start()
            return carry
        lax.fori_loop(0, zt_ref.shape[0], start_body, 0)

        def wait_body(z, carry):
            @pl.when(zt_ref[z] >= 0)
            def _():
                zero_copy(z).wait()
            return carry
        lax.fori_loop(0, zt_ref.shape[0], wait_body, 0)

    def row_copy(s, r, p):
        return pltpu.make_async_copy(stage_ref.at[s, pl.ds(r, 1), :], xs_ref.at[pl.ds(p, 1), :], sem.at[s])

    slot = i % 2
    stage_ref[slot] = tp_ref[...]
    base = i * (tr * TOP_K)

    def issue(r, carry):
        for kk in range(TOP_K):
            row_copy(slot, r, pos_ref[base + r * TOP_K + kk]).start()
        return carry
    lax.fori_loop(0, tr, issue, 0, unroll=4)

    def drain(s):
        def body(r, carry):
            for kk in range(TOP_K):
                row_copy(s, r, 0).wait()
            return carry
        lax.fori_loop(0, tr, body, 0)

    @pl.when(i > 0)
    def _():
        drain(1 - slot)

    @pl.when(i == n - 1)
    def _():
        drain(slot)


def _dispatch(pos_flat, zero_tiles, tpk, *, n_rows, tr, tile):
    t_rows = tpk.shape[0]
    return pl.pallas_call(
        functools.partial(_dispatch_kernel, tr=tr, tile=tile),
        grid_spec=pltpu.PrefetchScalarGridSpec(
            num_scalar_prefetch=2,
            grid=(t_rows // tr,),
            in_specs=[pl.BlockSpec((tr, HALF), lambda i, pos, zt: (i, 0))],
            out_specs=pl.BlockSpec(memory_space=pl.ANY),
            scratch_shapes=[
                pltpu.VMEM((2, tr, HALF), jnp.uint32),
                pltpu.VMEM((tile, HALF), jnp.uint32),
                pltpu.SemaphoreType.DMA((2,)),
                pltpu.SemaphoreType.DMA(()),
            ],
        ),
        out_shape=jax.ShapeDtypeStruct((n_rows, HALF), jnp.uint32),
        compiler_params=_cparams(1),
        name="dispatch",
    )(pos_flat, zero_tiles, tpk)


def _experts_kernel(te_ref, nv_ref, nu_ref, xs_ref, wg_ref, bg_ref, wu_ref, bu_ref, wd_ref, bd_ref,
                    o_ref, xb_ref, acc_ref, *, sub):
    del te_ref, nu_ref
    i = pl.program_id(0)
    j = pl.program_id(1)
    n_ff = pl.num_programs(1)
    tm = xs_ref.shape[0]
    valid = nv_ref[i]

    @pl.when(j == 0)
    def _():
        lo, hi = _unpack_rows(xs_ref[...])
        xb_ref[:, :HALF] = lo.astype(BF16)
        xb_ref[:, HALF:] = hi.astype(BF16)
        acc_ref[...] = jnp.broadcast_to(bd_ref[0], acc_ref.shape)

    def ffn_chunk(n_rows):
        xb = xb_ref[0:n_rows, :]
        g = jnp.dot(xb, wg_ref[0].astype(BF16), preferred_element_type=F32) + bg_ref[0]
        u = jnp.dot(xb, wu_ref[0].astype(BF16), preferred_element_type=F32) + bu_ref[0]
        g = jnp.minimum(g, SWIGLU_LIMIT)
        u = jnp.clip(u, -SWIGLU_LIMIT, SWIGLU_LIMIT)
        act = (u + 1.0) * g * jax.nn.sigmoid(SWIGLU_ALPHA * g)
        acc_ref[0:n_rows, :] += jnp.dot(act.astype(BF16), wd_ref[0].astype(BF16),
                                        preferred_element_type=F32)

    n_sub = tm // sub
    for s in range(n_sub):
        lo_rows, hi_rows = s * sub, (s + 1) * sub
        upper = (valid <= hi_rows) if s + 1 < n_sub else True

        @pl.when((valid > lo_rows) & upper)
        def _():
            ffn_chunk(hi_rows)

    @pl.when(j == n_ff - 1)
    def _():
        o_ref[...] = _pack_rows(acc_ref[...])


def _experts(tile_expert, tile_valid, n_used, xs, w_gate, b_gate, w_up, b_up, w_down, b_down,
             *, tm, tf, sub):
    n_tiles = xs.shape[0] // tm
    n_ff = D_FF // tf

    def tile_i(i, nu):
        return jnp.maximum(jnp.minimum(i, nu[0] - 1), 0)

    def ff_j(i, j, nu):
        return jnp.where(i < nu[0], j, n_ff - 1)

    return pl.pallas_call(
        functools.partial(_experts_kernel, sub=sub),
        grid_spec=pltpu.PrefetchScalarGridSpec(
            num_scalar_prefetch=3,
            grid=(n_tiles, n_ff),
            in_specs=[
                pl.BlockSpec((tm, HALF), lambda i, j, te, nv, nu: (tile_i(i, nu), 0)),
                pl.BlockSpec((1, D_MODEL, tf), lambda i, j, te, nv, nu: (te[i], 0, ff_j(i, j, nu))),
                pl.BlockSpec((1, 1, tf), lambda i, j, te, nv, nu: (te[i], 0, ff_j(i, j, nu))),
                pl.BlockSpec((1, D_MODEL, tf), lambda i, j, te, nv, nu: (te[i], 0, ff_j(i, j, nu))),
                pl.BlockSpec((1, 1, tf), lambda i, j, te, nv, nu: (te[i], 0, ff_j(i, j, nu))),
                pl.BlockSpec((1, tf, D_MODEL), lambda i, j, te, nv, nu: (te[i], ff_j(i, j, nu), 0)),
                pl.BlockSpec((1, 1, D_MODEL), lambda i, j, te, nv, nu: (te[i], 0, 0)),
            ],
            out_specs=pl.BlockSpec((tm, HALF), lambda i, j, te, nv, nu: (i, 0)),
            scratch_shapes=[
                pltpu.VMEM((tm, D_MODEL), BF16),
                pltpu.VMEM((tm, D_MODEL), F32),
            ],
        ),
        out_shape=jax.ShapeDtypeStruct(xs.shape, jnp.uint32),
        compiler_params=_cparams(2),
        name="experts",
    )(tile_expert, tile_valid, n_used, xs, w_gate, b_gate, w_up, b_up, w_down, b_down)


def _combine_kernel(pos_ref, h_ref, w_ref, ys_ref, o_ref, buf_ref, sem):
    i = pl.program_id(0)
    n = pl.num_programs(0)
    tc = h_ref.shape[0]

    def row_copy(p, slot, kk, r):
        return pltpu.make_async_copy(ys_ref.at[pl.ds(p, 1), :],
                                     buf_ref.at[slot, kk, pl.ds(r, 1), :], sem.at[slot])

    def issue(step, slot):
        base = step * (tc * TOP_K)

        def body(r, carry):
            for kk in range(TOP_K):
                row_copy(pos_ref[base + r * TOP_K + kk], slot, kk, r).start()
            return carry
        lax.fori_loop(0, tc, body, 0, unroll=4)

    @pl.when(i == 0)
    def _():
        issue(0, 0)

    @pl.when(i + 1 < n)
    def _():
        issue(i + 1, (i + 1) % 2)

    slot = i % 2

    def drain(r, carry):
        for kk in range(TOP_K):
            row_copy(0, slot, kk, r).wait()
        return carry
    lax.fori_loop(0, tc, drain, 0)

    w = w_ref[...]
    acc_lo = h_ref[:, :HALF]
    acc_hi = h_ref[:, HALF:]
    for kk in range(TOP_K):
        lo, hi = _unpack_rows(buf_ref[slot, kk])
        acc_lo = acc_lo + w[:, kk:kk + 1] * lo
        acc_hi = acc_hi + w[:, kk:kk + 1] * hi
    o_ref[:, :HALF] = acc_lo
    o_ref[:, HALF:] = acc_hi


def _combine(pos_flat, h2d, w_top, ys, *, tc):
    t_rows = h2d.shape[0]
    return pl.pallas_call(
        _combine_kernel,
        grid_spec=pltpu.PrefetchScalarGridSpec(
            num_scalar_prefetch=1,
            grid=(t_rows // tc,),
            in_specs=[
                pl.BlockSpec((tc, D_MODEL), lambda i, pos: (i, 0)),
                pl.BlockSpec((tc, TOP_K), lambda i, pos: (i, 0)),
                pl.BlockSpec(memory_space=pl.ANY),
            ],
            out_specs=pl.BlockSpec((tc, D_MODEL), lambda i, pos: (i, 0)),
            scratch_shapes=[
                pltpu.VMEM((2, TOP_K, tc, HALF), jnp.uint32),
                pltpu.SemaphoreType.DMA((2,)),
            ],
        ),
        out_shape=jax.ShapeDtypeStruct(h2d.shape, F32),
        compiler_params=_cparams(1),
        name="combine",
    )(pos_flat, h2d, w_top, ys)


def _pick(n, pref):
    t = min(n, pref)
    assert n % t == 0, (n, t)
    return t


def kernel(x, norm1_g, w_in, fox_f_bias, q_norm_g, k_norm_g, ml_i_bias, ml_f_bias, ml_out_norm_g,
           w_branch_a, w_branch_b, w_out, norm2_g, w_router, b_router,
           w_gate, b_gate, w_up, b_up, w_down, b_down):
    bsz, seq, d_model = x.shape
    assert d_model == D_MODEL and norm1_g.shape[0] == 1, "single-layer block of width 2048"
    t_rows = bsz * seq
    x2d = x.reshape(t_rows, D_MODEL).astype(F32)

    offs = [0]
    for wdt in IN_WIDTHS:
        offs.append(offs[-1] + wdt)
    seg = lambda n: w_in[0][:, offs[n]:offs[n + 1]]
    w_main = _repack_w_in(w_in[0].astype(F32), tn=512)
    w_gl = jnp.concatenate([seg(3), seg(7), seg(8),
                            jnp.zeros((D_MODEL, LANES - FOX_HEADS - 2 * ML_HEADS), F32)], axis=1).astype(BF16)
    colscale = jnp.ones((PROJ_W,), F32)
    colscale = colscale.at[COL_FQ:COL_FQ + FOX_W].set(
        jnp.tile(q_norm_g[0].astype(F32), FOX_HEADS) * (FOX_HEAD_DIM ** -0.5 * LOG2_E))
    colscale = colscale.at[COL_FK:COL_FK + FOX_W].set(jnp.tile(k_norm_g[0].astype(F32), FOX_HEADS))
    colscale = colscale.at[COL_MK:COL_MK + ML_QK_W].set(ML_QK_DIM ** -0.5)
    colscale = colscale.reshape(1, PROJ_W)
    gate_bias = jnp.concatenate([fox_f_bias[0], ml_i_bias[0], ml_f_bias[0],
                                 jnp.zeros((LANES - FOX_HEADS - 2 * ML_HEADS,), F32)]).astype(F32).reshape(1, LANES)
    g2 = norm2_g.astype(F32).reshape(1, D_MODEL)

    proj, gates = _in_proj(x2d, norm1_g.astype(F32).reshape(1, D_MODEL), w_main, w_gl, colscale,
                           tm=_pick(t_rows, 1024), tn=512)
    chunk = _pick(seq, 256)
    gp = _gate_prep(gates.reshape(bsz, seq, LANES), gate_bias, ts=_pick(seq, 512), chunk=chunk)
    gp_rows = jnp.transpose(gp[:, :, :16], (0, 2, 1))
    fox_group = 2
    neg_cum_f = (-LOG2_E * gp_rows[:, GCOL_FF:GCOL_FF + FOX_HEADS, :]).reshape(
        bsz * FOX_HEADS // fox_group, fox_group, seq)
    proj3d = proj.reshape(bsz, seq, PROJ_W)
    qk_bound = (1.02 * FOX_HEAD_DIM * (FOX_HEAD_DIM ** -0.5 * LOG2_E)
                * jnp.max(jnp.abs(q_norm_g[0].astype(F32))) * jnp.max(jnp.abs(k_norm_g[0].astype(F32))))
    y_a = _fox_attention(proj3d, neg_cum_f, qk_bound, tq=_pick(seq, 512), heads=fox_group)
    y_b = _mlstm(proj3d, gp, gp_rows, ml_out_norm_g.astype(F32).reshape(1, ML_V_W), chunk=chunk)

    w_r = jnp.concatenate([w_router[0].astype(F32), jnp.zeros((D_MODEL, LANES - N_EXPERTS), F32)], axis=1)
    b_r = jnp.concatenate([b_router[0].astype(F32), jnp.full((LANES - N_EXPERTS,), -jnp.inf, F32)]).reshape(1, LANES)
    h2d, tpk, route, counts = _merge(
        y_a.reshape(t_rows, FOX_W), y_b.reshape(t_rows, ML_V_W), proj, x2d,
        w_branch_a[0].astype(BF16), w_branch_b[0].astype(BF16), w_out[0].astype(BF16),
        g2, w_r, b_r, tm=_pick(t_rows, 512))

    tm_e = _pick(t_rows * TOP_K, 1024)
    n_tiles = (t_rows * TOP_K) // tm_e + N_EXPERTS
    top_idx = route[:, 0:TOP_K].astype(jnp.int32)
    top_rank = route[:, TOP_K:2 * TOP_K].astype(jnp.int32)
    top_w = route[:, 2 * TOP_K:3 * TOP_K]
    cnt = counts[0, :N_EXPERTS].astype(jnp.int32)
    tiles_per_e = (cnt + tm_e - 1) // tm_e
    tile_end = jnp.cumsum(tiles_per_e)
    tile_begin = tile_end - tiles_per_e
    pos_flat = ((tile_begin * tm_e)[top_idx] + top_rank).reshape(-1)
    n_used = tile_end[-1:].astype(jnp.int32)
    tile_ids = jnp.arange(n_tiles, dtype=jnp.int32)
    used_ids = jnp.minimum(tile_ids, n_used[0] - 1)
    tile_expert = jnp.sum((used_ids[:, None] >= tile_end[None, :]).astype(jnp.int32), axis=1)
    tile_valid = jnp.clip(cnt[tile_expert] - (used_ids - tile_begin[tile_expert]) * tm_e, 0, tm_e)
    tile_valid = jnp.where(tile_ids < n_used[0], tile_valid, 0).astype(jnp.int32)

    last_tile = jnp.where(tiles_per_e > 0, tile_end - 1, -1)
    tail_ids = tile_ids[n_tiles - N_EXPERTS:]
    zero_tiles = jnp.concatenate([last_tile, jnp.where(tail_ids >= n_used[0], tail_ids, -1)]).astype(jnp.int32)
    xs = _dispatch(pos_flat, zero_tiles, tpk, n_rows=n_tiles * tm_e, tr=_pick(t_rows, 256), tile=tm_e)
    ys = _experts(tile_expert, tile_valid, n_used, xs,
                  w_gate[0], b_gate[0].reshape(N_EXPERTS, 1, D_FF),
                  w_up[0], b_up[0].reshape(N_EXPERTS, 1, D_FF),
                  w_down[0], b_down[0].reshape(N_EXPERTS, 1, D_MODEL),
                  tm=tm_e, tf=256, sub=_pick(tm_e, 256))
    out = _combine(pos_flat, h2d, top_w, ys, tc=_pick(t_rows, 128))
    return out.reshape(bsz, seq, D_MODEL).astype(x.dtype)
```
